```python
import jax, jax.numpy as jnp
from jax import lax
import numpy as np

D_MODEL = 1024
BATCH = 8
SEQ = 4096
DEPTH = 2
DEC_BATCH = 32
DEC_SEQ = 32
PAST_LEN = 4096

CHUNK = 64
N_EVEN = (DEPTH + 1) // 2
N_ODD = DEPTH // 2
MIX_WIDTH = D_MODEL
SB_WIDTH = MIX_WIDTH // 2
SB_HEAD_DIM = 64
N_SB_HEADS = SB_WIDTH // SB_HEAD_DIM
SB_SCALE = SB_HEAD_DIM ** -0.5
Q_BLOCK = 128
D_RNN = MIX_WIDTH // 2
N_LRU_BLOCKS = 8
LRU_BLOCK = D_RNN // N_LRU_BLOCKS
CONV_W = 4
LRU_C = 8.0
IN_WIDTH = 3 * SB_WIDTH + 2 * D_RNN
POOL_WINDOWS = (2, 4, 8, 16)
N_POOL_GROUPS = len(POOL_WINDOWS)
POOL_GROUP = D_MODEL // N_POOL_GROUPS
POOL_BUF = max(POOL_WINDOWS) - 1
D_FF = -(-8 * D_MODEL // (3 * 256)) * 256
EPS = 1e-6

kernel_name = "stickbreak_rglru_pool_streaming_step"


def rmsnorm(x, g):
    xf = x.astype(jnp.float32)
    y = xf * lax.rsqrt(jnp.mean(xf * xf, axis=-1, keepdims=True) + EPS)
    return (y * g.astype(jnp.float32)).astype(x.dtype)


def swiglu(x, w_gate, w_up, w_down):
    return (jax.nn.silu(x @ w_gate) * (x @ w_up)) @ w_down


def sb_block(q, k, v, q_pos, k_pos):
    z = jnp.einsum('bqhd,bkhd->bhqk', q, k).astype(jnp.float32) * SB_SCALE
    before = (k_pos[None, :] < q_pos[:, None])[None, None]
    log_not = jnp.where(before, jax.nn.log_sigmoid(-z), 0.0)
    between = lax.cumsum(log_not, axis=3, reverse=True) - log_not
    w = jnp.where(before, jnp.exp(jax.nn.log_sigmoid(z) + between), 0.0)
    return jnp.einsum('bhqk,bkhd->bqhd', w.astype(v.dtype), v)


def sb_prompt(q, k, v):
    b, s = q.shape[0], q.shape[1]
    nb = s // Q_BLOCK
    qb = q.reshape(b, nb, Q_BLOCK, N_SB_HEADS, SB_HEAD_DIM).swapaxes(0, 1)
    pos_b = jnp.arange(s).reshape(nb, Q_BLOCK)
    k_pos = jnp.arange(s)
    out = lax.map(lambda a: sb_block(a[0], k, v, a[1], k_pos), (qb, pos_b))
    return out.swapaxes(0, 1).reshape(b, s, N_SB_HEADS, SB_HEAD_DIM)


def causal_conv(u, buf, w, b):
    t = u.shape[1]
    up = jnp.concatenate([buf.astype(u.dtype), u], axis=1)
    y = b + sum(up[:, i:i + t] * w[i] for i in range(CONV_W))
    return y, up[:, -(CONV_W - 1):]


def rg_lru(u, h0, rg_w, rg_b, ig_w, ig_b, lam):
    b, t, _ = u.shape
    ub = u.reshape(b, t, N_LRU_BLOCKS, LRU_BLOCK)
    r = jax.nn.sigmoid(jnp.einsum('btnc,ncd->btnd', ub, rg_w).reshape(b, t, D_RNN) + rg_b)
    i = jax.nn.sigmoid(jnp.einsum('btnc,ncd->btnd', ub, ig_w).reshape(b, t, D_RNN) + ig_b)
    log_a = (-LRU_C * r * jax.nn.softplus(-lam)).astype(jnp.float32)
    a = jnp.exp(log_a)
    x_in = jnp.sqrt(-jnp.expm1(2.0 * log_a)) * (i * u).astype(jnp.float32)

    def combine(left, right):
        a1, b1 = left
        a2, b2 = right
        return a1 * a2, a2 * b1 + b2

    a_cum, h_zero = lax.associative_scan(combine, (a, x_in), axis=1)
    h = h_zero + a_cum * h0[:, None].astype(jnp.float32)
    return h.astype(u.dtype), h[:, -1].astype(u.dtype)


def hybrid_mixer(xn, past_k, past_v, h0, conv_buf, w_in, conv_w, conv_b, rg_w, rg_b, ig_w, ig_b, lam, w_out):
    b, t, _ = xn.shape
    proj = xn @ w_in
    q, k, v, u, g = jnp.split(proj, [SB_WIDTH, 2 * SB_WIDTH, 3 * SB_WIDTH, 3 * SB_WIDTH + D_RNN], axis=-1)
    q = q.reshape(b, t, N_SB_HEADS, SB_HEAD_DIM)
    k = k.reshape(b, t, N_SB_HEADS, SB_HEAD_DIM)
    v = v.reshape(b, t, N_SB_HEADS, SB_HEAD_DIM)
    if past_k is None:
        attn = sb_prompt(q, k, v)
    else:
        p = past_k.shape[1]
        k_all = jnp.concatenate([past_k.astype(k.dtype), k], axis=1)
        v_all = jnp.concatenate([past_v.astype(v.dtype), v], axis=1)
        attn = sb_block(q, k_all, v_all, p + jnp.arange(t), jnp.arange(p + t))
    uc, conv_new = causal_conv(u, conv_buf, conv_w, conv_b)
    h, h_last = rg_lru(uc, h0, rg_w, rg_b, ig_w, ig_b, lam)
    lru_out = h * jax.nn.gelu(g)
    y = jnp.concatenate([attn.reshape(b, t, SB_WIDTH), lru_out], axis=-1) @ w_out
    return y.astype(xn.dtype), k, v, h_last, conv_new


def pool_mixer(xn, buf, start_pos, pool_w, pool_scale):
    b, t, _ = xn.shape
    xcat = jnp.concatenate([buf.astype(xn.dtype), xn], axis=1)
    xf = xcat.astype(jnp.float32)
    cs = jnp.concatenate([jnp.zeros((b, 1, D_MODEL), jnp.float32), jnp.cumsum(xf, axis=1)], axis=1)
    pos = start_pos + jnp.arange(t)
    outs = []
    for gi, win in enumerate(POOL_WINDOWS):
        sl = slice(gi * POOL_GROUP, (gi + 1) * POOL_GROUP)
        end = cs[:, POOL_BUF + 1:POOL_BUF + 1 + t, sl]
        begin = cs[:, POOL_BUF + 1 - win:POOL_BUF + 1 - win + t, sl]
        cnt = jnp.minimum(win, pos + 1).astype(jnp.float32)[None, :, None]
        outs.append((end - begin) / cnt - xf[:, POOL_BUF:, sl])
    d = jnp.stack(outs, axis=2)
    y = jnp.einsum('btgc,gcd->btgd', d, pool_w.astype(jnp.float32)).reshape(b, t, D_MODEL)
    y = y * pool_scale.astype(jnp.float32)
    return y.astype(xn.dtype), xcat[:, -POOL_BUF:]


def trunk(x, past_k, past_v, h0, conv0, pool0, start_pos,
          hyb_w_in, hyb_conv_w, hyb_conv_b, hyb_rg_w, hyb_rg_b, hyb_ig_w, hyb_ig_b, hyb_lambda, hyb_w_out,
          pool_w, pool_scale, norm_mix, norm_ffn, ffn_gate, ffn_up, ffn_down, norm_final):
    ks, vs, hs, cs, ps = [], [], [], [], []
    for layer in range(DEPTH):
        j = layer // 2
        xn = rmsnorm(x, norm_mix[layer])
        if layer % 2 == 0:
            pk = None if past_k is None else past_k[j]
            pv = None if past_v is None else past_v[j]
            y, k_new, v_new, h_new, c_new = hybrid_mixer(
                xn, pk, pv, h0[j], conv0[j], hyb_w_in[j], hyb_conv_w[j], hyb_conv_b[j],
                hyb_rg_w[j], hyb_rg_b[j], hyb_ig_w[j], hyb_ig_b[j], hyb_lambda[j], hyb_w_out[j])
            ks.append(k_new)
            vs.append(v_new)
            hs.append(h_new)
            cs.append(c_new)
        else:
            y, p_new = pool_mixer(xn, pool0[j], start_pos, pool_w[j], pool_scale[j])
            ps.append(p_new)
        x = x + y
        x = x + swiglu(rmsnorm(x, norm_ffn[layer]), ffn_gate[layer], ffn_up[layer], ffn_down[layer]).astype(x.dtype)
    return (rmsnorm(x, norm_final), jnp.stack(ks), jnp.stack(vs), jnp.stack(hs), jnp.stack(cs), jnp.stack(ps))


def setup_inputs(seed: int = 0) -> dict:
    key = jax.random.key(seed)
    ks = jax.random.split(key, 32)

    def nrm(k, shape, scale=1.0):
        return jax.random.normal(k, shape, jnp.float32) * scale

    a0 = jax.random.uniform(ks[14], (N_EVEN, D_RNN), jnp.float32, minval=0.9, maxval=0.999) ** (1.0 / LRU_C)
    return {
        "x_prompt": nrm(ks[0], (BATCH, SEQ, D_MODEL)),
        "x_sample": nrm(ks[1], (DEC_BATCH, DEC_SEQ, D_MODEL)),
        "cache_sb_k": nrm(ks[2], (N_EVEN, DEC_BATCH, PAST_LEN, N_SB_HEADS, SB_HEAD_DIM)),
        "cache_sb_v": nrm(ks[3], (N_EVEN, DEC_BATCH, PAST_LEN, N_SB_HEADS, SB_HEAD_DIM)),
        "state_lru_h": nrm(ks[4], (N_EVEN, DEC_BATCH, D_RNN), 0.5),
        "state_lru_conv": nrm(ks[5], (N_EVEN, DEC_BATCH, CONV_W - 1, D_RNN)),
        "state_pool": nrm(ks[6], (N_ODD, DEC_BATCH, POOL_BUF, D_MODEL)),
        "hyb_w_in": nrm(ks[7], (N_EVEN, D_MODEL, IN_WIDTH), D_MODEL ** -0.5),
        "hyb_conv_w": nrm(ks[8], (N_EVEN, CONV_W, D_RNN), CONV_W ** -0.5),
        "hyb_conv_b": nrm(ks[9], (N_EVEN, D_RNN), 0.01),
        "hyb_rg_w": nrm(ks[10], (N_EVEN, N_LRU_BLOCKS, LRU_BLOCK, LRU_BLOCK), LRU_BLOCK ** -0.5),
        "hyb_rg_b": nrm(ks[11], (N_EVEN, D_RNN), 0.01),
        "hyb_ig_w": nrm(ks[12], (N_EVEN, N_LRU_BLOCKS, LRU_BLOCK, LRU_BLOCK), LRU_BLOCK ** -0.5),
        "hyb_ig_b": nrm(ks[13], (N_EVEN, D_RNN), 0.01),
        "hyb_lambda": jnp.log(a0) - jnp.log1p(-a0),
        "hyb_w_out": nrm(ks[15], (N_EVEN, MIX_WIDTH, D_MODEL), MIX_WIDTH ** -0.5),
        "pool_w": nrm(ks[16], (N_ODD, N_POOL_GROUPS, POOL_GROUP, POOL_GROUP), POOL_GROUP ** -0.5),
        "pool_scale": 1.0 + nrm(ks[17], (N_ODD, D_MODEL), 0.1),
        "norm_mix": 1.0 + nrm(ks[18], (DEPTH, D_MODEL), 0.1),
        "norm_ffn": 1.0 + nrm(ks[19], (DEPTH, D_MODEL), 0.1),
        "ffn_gate": nrm(ks[20], (DEPTH, D_MODEL, D_FF), D_MODEL ** -0.5),
        "ffn_up": nrm(ks[21], (DEPTH, D_MODEL, D_FF), D_MODEL ** -0.5),
        "ffn_down": nrm(ks[22], (DEPTH, D_FF, D_MODEL), D_FF ** -0.5),
        "norm_final": 1.0 + nrm(ks[23], (D_MODEL,), 0.1),
    }


def reference(x_prompt, x_sample, cache_sb_k, cache_sb_v, state_lru_h, state_lru_conv, state_pool,
              hyb_w_in, hyb_conv_w, hyb_conv_b, hyb_rg_w, hyb_rg_b, hyb_ig_w, hyb_ig_b, hyb_lambda, hyb_w_out,
              pool_w, pool_scale, norm_mix, norm_ffn, ffn_gate, ffn_up, ffn_down, norm_final):
    b = x_prompt.shape[0]
    dt = x_prompt.dtype
    h0_p = jnp.zeros((N_EVEN, b, D_RNN), dt)
    conv0_p = jnp.zeros((N_EVEN, b, CONV_W - 1, D_RNN), dt)
    pool0_p = jnp.zeros((N_ODD, b, POOL_BUF, D_MODEL), dt)
    y_prompt, k_p, v_p, h_p, conv_p, pool_p = trunk(
        x_prompt, None, None, h0_p, conv0_p, pool0_p, 0,
        hyb_w_in, hyb_conv_w, hyb_conv_b, hyb_rg_w, hyb_rg_b, hyb_ig_w, hyb_ig_b, hyb_lambda, hyb_w_out,
        pool_w, pool_scale, norm_mix, norm_ffn, ffn_gate, ffn_up, ffn_down, norm_final)
    y_sample, k_s, v_s, h_s, conv_s, pool_s = trunk(
        x_sample, cache_sb_k, cache_sb_v, state_lru_h, state_lru_conv, state_pool, cache_sb_k.shape[2],
        hyb_w_in, hyb_conv_w, hyb_conv_b, hyb_rg_w, hyb_rg_b, hyb_ig_w, hyb_ig_b, hyb_lambda, hyb_w_out,
        pool_w, pool_scale, norm_mix, norm_ffn, ffn_gate, ffn_up, ffn_down, norm_final)
    return (y_prompt, y_sample, k_p, v_p, h_p, conv_p, pool_p, k_s, v_s, h_s, conv_s, pool_s)
```

```python
import functools

import jax
import jax.numpy as jnp
from jax import lax
from jax.experimental import pallas as pl
from jax.experimental.pallas import tpu as pltpu

F32 = jnp.float32
BF16 = jnp.bfloat16

N_SB_HEADS = 8
SB_HEAD_DIM = 64
SB_WIDTH = N_SB_HEADS * SB_HEAD_DIM
SB_SCALE = SB_HEAD_DIM ** -0.5
N_LRU_BLOCKS = 8
CONV_W = 4
LRU_C = 8.0
POOL_WINDOWS = (2, 4, 8, 16)
POOL_HIST = 16
EPS = 1e-6

LANES = 128
SUBLANES = 8
HEAD_PAIRS = SB_WIDTH // LANES
KEY_BLOCK = LANES
EXP_ZERO_BELOW = -104.0
VMEM_LIMIT_BYTES = 56 * 1024 * 1024


def _cparams(n_axes):
    return pltpu.CompilerParams(dimension_semantics=("arbitrary",) * n_axes,
                                vmem_limit_bytes=VMEM_LIMIT_BYTES)


def _resident(shape):
    zeros = (0,) * len(shape)
    return pl.BlockSpec(shape, lambda *_: zeros, pipeline_mode=pl.Buffered(1))


def _rms(x, gain):
    ms = jnp.mean(x * x, axis=-1, keepdims=True)
    return (x * lax.rsqrt(ms + EPS)) * gain


def _dot(a, b):
    return jnp.dot(a, b, preferred_element_type=F32)


def _in_proj_kernel(x_ref, gain_ref, w_ref, q_ref, k_ref, v_ref, kb_ref, vb_ref, u_ref, g_ref):
    xb = _rms(x_ref[...], gain_ref[...]).astype(BF16)
    w = SB_WIDTH
    q_ref[...] = (_dot(xb, w_ref[:, 0:w]) * SB_SCALE).astype(BF16)
    k = _dot(xb, w_ref[:, w:2 * w])
    k_ref[...] = k
    kb_ref[...] = k.astype(BF16)
    v = _dot(xb, w_ref[:, 2 * w:3 * w])
    v_ref[...] = v
    vb_ref[...] = v.astype(BF16)
    d_rnn = u_ref.shape[-1]
    u_ref[...] = _dot(xb, w_ref[:, 3 * w:3 * w + d_rnn])
    g_ref[...] = _dot(xb, w_ref[:, 3 * w + d_rnn:3 * w + 2 * d_rnn])


def _in_proj(x2d, gain, w_in, tm):
    rows, d = x2d.shape
    d_rnn = (w_in.shape[1] - 3 * SB_WIDTH) // 2
    row = lambda width: pl.BlockSpec((tm, width), lambda i: (i, 0))
    return pl.pallas_call(
        _in_proj_kernel,
        grid=(rows // tm,),
        in_specs=[row(d), _resident((1, d)), _resident(w_in.shape)],
        out_specs=[row(SB_WIDTH)] * 5 + [row(d_rnn)] * 2,
        out_shape=[jax.ShapeDtypeStruct((rows, SB_WIDTH), BF16),
                   jax.ShapeDtypeStruct((rows, SB_WIDTH), F32),
                   jax.ShapeDtypeStruct((rows, SB_WIDTH), F32),
                   jax.ShapeDtypeStruct((rows, SB_WIDTH), BF16),
                   jax.ShapeDtypeStruct((rows, SB_WIDTH), BF16),
                   jax.ShapeDtypeStruct((rows, d_rnn), F32),
                   jax.ShapeDtypeStruct((rows, d_rnn), F32)],
        compiler_params=_cparams(1),
        name="in_proj",
    )(x2d, gain, w_in)


def _cumsum_matrix():
    s = lax.broadcasted_iota(jnp.int32, (KEY_BLOCK, 2 * KEY_BLOCK), 0)
    j = lax.broadcasted_iota(jnp.int32, (KEY_BLOCK, 2 * KEY_BLOCK), 1)
    return ((s > j) | (j >= KEY_BLOCK)).astype(BF16)


def _sb_tile(q_ref, kblk, vblk, cm, acc_ref, car_ref, diag):
    tq = q_ref.shape[0]
    lane = lax.broadcasted_iota(jnp.int32, (tq, LANES), 1)
    low_half = lane < SB_HEAD_DIM
    if diag:
        before = lane < lax.broadcasted_iota(jnp.int32, (tq, KEY_BLOCK), 0)
    for p in range(HEAD_PAIRS):
        cols = slice(p * LANES, (p + 1) * LANES)
        q2, k2, v2 = q_ref[:, cols], kblk[:, cols], vblk[:, cols]
        acc = acc_ref[p]
        for par in range(2):
            head = 2 * p + par
            mine = low_half if par == 0 else jnp.logical_not(low_half)
            qm = jnp.where(mine, q2, jnp.zeros_like(q2))
            z = lax.dot_general(qm, k2, (((1,), (1,)), ((), ())), preferred_element_type=F32)
            soft = jnp.log(1.0 + jnp.exp(-jnp.abs(z)))
            log_beta = jnp.minimum(z, 0.0) - soft
            log_not = log_beta - z
            if diag:
                log_not = jnp.where(before, log_not, 0.0)
            hi = log_not.astype(BF16)
            lo = (log_not - hi.astype(F32)).astype(BF16)
            sums = _dot(hi, cm) + _dot(lo, cm)
            car = car_ref[head]
            wgt = jnp.exp(log_beta + sums[:, :KEY_BLOCK] + car)
            if diag:
                wgt = jnp.where(before, wgt, 0.0)
            car_ref[head] = car + sums[:, KEY_BLOCK:]
            pv = _dot(wgt.astype(BF16), v2)
            acc = acc + jnp.where(mine, pv, 0.0)
        acc_ref[p] = acc


def _sb_all_faded(car_ref):
    return jnp.max(car_ref[...]) < EXP_ZERO_BELOW


def _sb_write(o_ref, acc_ref):
    for p in range(HEAD_PAIRS):
        o_ref[:, p * LANES:(p + 1) * LANES] = acc_ref[p].astype(o_ref.dtype)


def _sb_prompt_kernel(q_ref, k_ref, v_ref, o_ref, acc_ref, car_ref):
    i = pl.program_id(1)
    cm = _cumsum_matrix()
    acc_ref[...] = jnp.zeros_like(acc_ref)
    car_ref[...] = jnp.zeros_like(car_ref)

    def keys(blk):
        start = pl.multiple_of(blk * KEY_BLOCK, KEY_BLOCK)
        return k_ref[pl.ds(start, KEY_BLOCK), :], v_ref[pl.ds(start, KEY_BLOCK), :]

    kblk, vblk = keys(i)
    _sb_tile(q_ref, kblk, vblk, cm, acc_ref, car_ref, diag=True)

    def more(state):
        blk, faded = state
        return jnp.logical_and(blk >= 0, jnp.logical_not(faded))

    def step(state):
        blk, _ = state
        kblk, vblk = keys(blk)
        _sb_tile(q_ref, kblk, vblk, cm, acc_ref, car_ref, diag=False)
        return blk - 1, _sb_all_faded(car_ref)

    lax.while_loop(more, step, (i - 1, _sb_all_faded(car_ref)))
    _sb_write(o_ref, acc_ref)


def _sb_prompt(q, kb, vb, batch, seq):
    assert seq % KEY_BLOCK == 0
    nq = seq // KEY_BLOCK
    qspec = pl.BlockSpec((KEY_BLOCK, SB_WIDTH), lambda b, i: (b * nq + i, 0))
    kvspec = pl.BlockSpec((seq, SB_WIDTH), lambda b, i: (b, 0))
    return pl.pallas_call(
        _sb_prompt_kernel,
        grid=(batch, nq),
        in_specs=[qspec, kvspec, kvspec],
        out_specs=qspec,
        out_shape=jax.ShapeDtypeStruct(q.shape, BF16),
        scratch_shapes=[pltpu.VMEM((HEAD_PAIRS, KEY_BLOCK, LANES), F32),
                        pltpu.VMEM((N_SB_HEADS, KEY_BLOCK, LANES), F32)],
        compiler_params=_cparams(2),
        name="sb_prompt",
    )(q, kb, vb)


def _sb_sample_kernel(q_ref, kn_ref, vn_ref, pk_ref, pv_ref, o_ref, acc_ref, car_ref, faded_ref):
    j = pl.program_id(1)
    cm = _cumsum_matrix()

    @pl.when(j == 0)
    def _():
        acc_ref[...] = jnp.zeros_like(acc_ref)
        car_ref[...] = jnp.zeros_like(car_ref)
        _sb_tile(q_ref, kn_ref[...], vn_ref[...], cm, acc_ref, car_ref, diag=True)
        faded_ref[0] = _sb_all_faded(car_ref).astype(jnp.int32)

    @pl.when(faded_ref[0] == 0)
    def _():
        for sub in reversed(range(pk_ref.shape[0] // KEY_BLOCK)):
            rows = slice(sub * KEY_BLOCK, (sub + 1) * KEY_BLOCK)
            _sb_tile(q_ref, pk_ref[rows, :].astype(BF16), pv_ref[rows, :].astype(BF16), cm,
                     acc_ref, car_ref, diag=False)
        faded_ref[0] = _sb_all_faded(car_ref).astype(jnp.int32)

    @pl.when(j == pl.num_programs(1) - 1)
    def _():
        _sb_write(o_ref, acc_ref)


def _sb_sample(q, kb, vb, past_k, past_v, batch, seq, past_block):
    past = past_k.shape[1]
    assert seq <= KEY_BLOCK and seq % 16 == 0 and past % past_block == 0
    npb = past // past_block
    pad = ((0, 0), (0, KEY_BLOCK - seq), (0, 0))
    kn = jnp.pad(kb.reshape(batch, seq, SB_WIDTH), pad).reshape(batch * KEY_BLOCK, SB_WIDTH)
    vn = jnp.pad(vb.reshape(batch, seq, SB_WIDTH), pad).reshape(batch * KEY_BLOCK, SB_WIDTH)
    qspec = pl.BlockSpec((seq, SB_WIDTH), lambda b, j: (b, 0))
    nspec = pl.BlockSpec((KEY_BLOCK, SB_WIDTH), lambda b, j: (b, 0))
    pspec = pl.BlockSpec((None, past_block, SB_WIDTH), lambda b, j: (b, npb - 1 - j, 0))
    return pl.pallas_call(
        _sb_sample_kernel,
        grid=(batch, npb),
        in_specs=[qspec, nspec, nspec, pspec, pspec],
        out_specs=qspec,
        out_shape=jax.ShapeDtypeStruct(q.shape, BF16),
        scratch_shapes=[pltpu.VMEM((HEAD_PAIRS, seq, LANES), F32),
                        pltpu.VMEM((N_SB_HEADS, seq, LANES), F32),
                        pltpu.SMEM((1,), jnp.int32)],
        compiler_params=_cparams(2),
        name="sb_sample",
    )(q, kn, vn, past_k, past_v)


def _shift_rows(x, d, fill):
    row = lax.broadcasted_iota(jnp.int32, x.shape, 0)
    return jnp.where((row % SUBLANES) >= d, pltpu.roll(x, d, 0), fill)


def _lru_kernel(u_ref, g_ref, h0_ref, c0_ref, cw_ref, cb_ref, gw_ref, rb_ref, ib_ref, lam_ref,
                o_ref, hl_ref, cn_ref, h_ref, tail_ref):
    i = pl.program_id(1)
    tc, d_rnn = u_ref.shape

    @pl.when(i == 0)
    def _():
        h_ref[...] = jnp.broadcast_to(h0_ref[...], h_ref.shape)
        tail_ref[...] = c0_ref[...]

    u = u_ref[...]
    ext = jnp.concatenate([tail_ref[...], u], axis=0)
    tail_ref[...] = ext[tc:, :]
    uc = cb_ref[...] + u * cw_ref[CONV_W - 1:CONV_W, :]
    for back in range(1, CONV_W):
        uc = uc + ext[SUBLANES - back:SUBLANES - back + tc, :] * cw_ref[CONV_W - 1 - back:CONV_W - back, :]

    ucb = uc.astype(BF16)
    gates = [_dot(ucb[:, p * LANES:(p + 1) * LANES], gw_ref[p]) for p in range(d_rnn // LANES)]
    r = jax.nn.sigmoid(jnp.concatenate([gt[:, :LANES] for gt in gates], axis=1) + rb_ref[...])
    gi = jax.nn.sigmoid(jnp.concatenate([gt[:, LANES:] for gt in gates], axis=1) + ib_ref[...])
    lam = lam_ref[...]
    softplus_neg = jnp.maximum(-lam, 0.0) + jnp.log1p(jnp.exp(-jnp.abs(lam)))
    log_a = (-LRU_C * r) * softplus_neg
    a = jnp.exp(log_a)
    th = jnp.tanh(log_a)
    x_in = jnp.sqrt((-2.0 * th) / (1.0 - th)) * (gi * uc)

    d = 1
    while d < SUBLANES:
        x_in = a * _shift_rows(x_in, d, 0.0) + x_in
        a = a * _shift_rows(a, d, 1.0)
        d *= 2
    h = h_ref[0:1, :]
    rows = []
    for grp in range(tc // SUBLANES):
        sl = slice(grp * SUBLANES, (grp + 1) * SUBLANES)
        hg = x_in[sl, :] + a[sl, :] * h
        rows.append(hg)
        h = hg[SUBLANES - 1:SUBLANES, :]
    h_ref[...] = jnp.broadcast_to(h, h_ref.shape)
    h_all = jnp.concatenate(rows, axis=0)

    g = g_ref[...]
    gelu = 0.5 * g * (1.0 + jnp.tanh(0.7978845608028654 * (g + 0.044715 * (g * g * g))))
    o_ref[...] = (h_all * gelu).astype(o_ref.dtype)

    @pl.when(i == pl.num_programs(1) - 1)
    def _():
        hl_ref[...] = h
        cn_ref[...] = ext[tc:, :]


def _lru(u, g, h0, conv0, conv_w, conv_b, gate_w, rg_b, ig_b, lam, batch, seq, tc):
    d_rnn = u.shape[1]
    assert seq % tc == 0 and tc % SUBLANES == 0 and seq >= CONV_W - 1
    nt = seq // tc
    conv0p = jnp.pad(conv0, ((0, 0), (SUBLANES - (CONV_W - 1), 0), (0, 0)))
    row = pl.BlockSpec((tc, d_rnn), lambda b, i: (b * nt + i, 0))
    per_batch = lambda n: pl.BlockSpec((None, n, d_rnn), lambda b, i: (b, 0, 0))
    vec = _resident((1, d_rnn))
    out, h_last, conv_new = pl.pallas_call(
        _lru_kernel,
        grid=(batch, nt),
        in_specs=[row, row, per_batch(1), per_batch(SUBLANES), _resident(conv_w.shape), vec,
                  _resident(gate_w.shape), vec, vec, vec],
        out_specs=[row, per_batch(1), per_batch(SUBLANES)],
        out_shape=[jax.ShapeDtypeStruct(u.shape, BF16),
                   jax.ShapeDtypeStruct((batch, 1, d_rnn), F32),
                   jax.ShapeDtypeStruct((batch, SUBLANES, d_rnn), F32)],
        scratch_shapes=[pltpu.VMEM((SUBLANES, d_rnn), F32), pltpu.VMEM((SUBLANES, d_rnn), F32)],
        compiler_params=_cparams(2),
        name="rg_lru",
    )(u, g, h0.reshape(batch, 1, d_rnn), conv0p, conv_w, conv_b, gate_w, rg_b, ig_b, lam)
    return out, h_last.reshape(batch, d_rnn), conv_new[:, SUBLANES - (CONV_W - 1):, :]


def _ffn(xb, wg_ref, wu_ref, wd_ref, chunk):
    d_ff = wg_ref.shape[1]
    acc = None
    for c in range(0, d_ff, chunk):
        gate = _dot(xb, wg_ref[:, c:c + chunk])
        up = _dot(xb, wu_ref[:, c:c + chunk])
        hid = (gate * jax.nn.sigmoid(gate) * up).astype(BF16)
        part = _dot(hid, wd_ref[c:c + chunk, :])
        acc = part if acc is None else acc + part
    return acc


def _ffn_chunk(d_ff):
    for chunk in (512, 256, 128):
        if d_ff % chunk == 0:
            return chunk
    return d_ff


def _mix_ffn_kernel(a_ref, l_ref, x_ref, wo_ref, gain_ref, wg_ref, wu_ref, wd_ref, o_ref, *, chunk):
    sbw = a_ref.shape[1]
    y = _dot(a_ref[...], wo_ref[0:sbw, :]) + _dot(l_ref[...], wo_ref[sbw:, :])
    x1 = x_ref[...] + y
    xb = _rms(x1, gain_ref[...]).astype(BF16)
    o_ref[...] = x1 + _ffn(xb, wg_ref, wu_ref, wd_ref, chunk)


def _mix_ffn(attn, lru_out, x2d, w_out, gain, wg, wu, wd, tm):
    rows, d = x2d.shape
    row = lambda width: pl.BlockSpec((tm, width), lambda i: (i, 0))
    return pl.pallas_call(
        functools.partial(_mix_ffn_kernel, chunk=_ffn_chunk(wg.shape[1])),
        grid=(rows // tm,),
        in_specs=[row(attn.shape[1]), row(lru_out.shape[1]), row(d), _resident(w_out.shape),
                  _resident((1, d)), _resident(wg.shape), _resident(wu.shape), _resident(wd.shape)],
        out_specs=row(d),
        out_shape=jax.ShapeDtypeStruct((rows, d), F32),
        compiler_params=_cparams(1),
        name="mix_ffn",
    )(attn, lru_out, x2d, w_out, gain, wg, wu, wd)


def _pool_delta(ext, n, pos0):
    d = ext.shape[1]
    gw = d // len(POOL_WINDOWS)
    pos = pos0 + lax.broadcasted_iota(jnp.int32, (n, 1), 0)
    cur = ext[POOL_HIST:, :]
    sums, s, col0 = [], ext, 0
    for w in POOL_WINDOWS:
        half = w // 2
        s = s[half:, col0:] + s[:-half, col0:]
        first = POOL_HIST - (w - 1)
        sums.append(s[first:first + n, :gw])
        col0 = gw
    outs = []
    for g, w in enumerate(POOL_WINDOWS):
        inv = 1.0 / jnp.minimum(w, pos + 1).astype(F32)
        outs.append(sums[g] * inv - cur[:, g * gw:(g + 1) * gw])
    return outs


def _pool_ffn_kernel(x_ref, buf_ref, gm_ref, pw_ref, ps_ref, gf_ref, wg_ref, wu_ref, wd_ref,
                     gl_ref, o_ref, pn_ref, hist_ref, d_ref, *, chunk, segs_per_batch, start_pos):
    nb, seg, d = x_ref.shape
    gw = d // len(POOL_WINDOWS)
    i = pl.program_id(0)
    x = x_ref[...].reshape(nb * seg, d)
    xn = _rms(x, gm_ref[...])
    if segs_per_batch == 1:
        seg_idx = 0
    else:
        seg_idx = i % segs_per_batch

        @pl.when(seg_idx == 0)
        def _():
            hist_ref[...] = buf_ref[0]

    for b in range(nb):
        hist = buf_ref[b] if segs_per_batch == 1 else hist_ref[...]
        ext = jnp.concatenate([hist, xn[b * seg:(b + 1) * seg, :]], axis=0)
        for g, dg in enumerate(_pool_delta(ext, seg, start_pos + seg_idx * seg)):
            d_ref[b * seg:(b + 1) * seg, g * gw:(g + 1) * gw] = dg.astype(BF16)
        tail = ext[seg:, :]
        if segs_per_batch == 1:
            pn_ref[b] = tail
        else:
            hist_ref[...] = tail

            @pl.when(seg_idx == segs_per_batch - 1)
            def _():
                pn_ref[0] = tail

    y = jnp.concatenate([_dot(d_ref[:, g * gw:(g + 1) * gw], pw_ref[g])
                         for g in range(len(POOL_WINDOWS))], axis=1)
    x1 = x + y * ps_ref[...]
    xb = _rms(x1, gf_ref[...]).astype(BF16)
    x2 = x1 + _ffn(xb, wg_ref, wu_ref, wd_ref, chunk)
    o_ref[...] = _rms(x2, gl_ref[...]).reshape(nb, seg, d)


def _pool_ffn(x2d, pool0, start_pos, gain_mix, pool_w, pool_scale, gain_ffn, wg, wu, wd,
              gain_final, batch, seq, seg, nb):
    d = x2d.shape[1]
    spb = seq // seg
    assert seq % seg == 0 and seg % SUBLANES == 0 and (nb == 1 or spb == 1) and batch % nb == 0
    nseg = batch * spb
    buf = jnp.pad(pool0, ((0, 0), (1, 0), (0, 0)))
    xspec = pl.BlockSpec((nb, seg, d), lambda i: (i, 0, 0))
    bspec = pl.BlockSpec((nb, POOL_HIST, d), (lambda i: (i, 0, 0)) if spb == 1 else (lambda i: (i // spb, 0, 0)))
    vec = _resident((1, d))
    y, pool_new = pl.pallas_call(
        functools.partial(_pool_ffn_kernel, chunk=_ffn_chunk(wg.shape[1]), segs_per_batch=spb,
                          start_pos=start_pos),
        grid=(nseg // nb,),
        in_specs=[xspec, bspec, vec, _resident(pool_w.shape), vec, vec, _resident(wg.shape),
                  _resident(wu.shape), _resident(wd.shape), vec],
        out_specs=[xspec, bspec],
        out_shape=[jax.ShapeDtypeStruct((nseg, seg, d), F32),
                   jax.ShapeDtypeStruct((batch, POOL_HIST, d), F32)],
        scratch_shapes=[pltpu.VMEM((POOL_HIST, d), F32), pltpu.VMEM((nb * seg, d), BF16)],
        compiler_params=_cparams(1),
        name="pool_ffn",
    )(x2d.reshape(nseg, seg, d), buf, gain_mix, pool_w, pool_scale, gain_ffn, wg, wu, wd, gain_final)
    return y.reshape(batch * seq, d), pool_new[:, 1:, :]


def _pick(n, candidates):
    for c in candidates:
        if n % c == 0:
            return c
    return n


def _gate_pairs(rg_w, ig_w):
    nblk, c, _ = rg_w.shape
    z = jnp.zeros((c, c), rg_w.dtype)

    def pair(w, p):
        return jnp.block([[w[2 * p], z], [z, w[2 * p + 1]]])

    return jnp.stack([jnp.concatenate([pair(rg_w, p), pair(ig_w, p)], axis=1)
                      for p in range(nblk // 2)]).astype(BF16)


def _trunk(x, past_k, past_v, h0, conv0, pool0, start_pos, w):
    batch, seq, d = x.shape
    rows = batch * seq
    x2d = x.reshape(rows, d)
    tm = _pick(rows, (512, 256, 128, 64, 32, 16, 8))

    q, k, v, kb, vb, u, g = _in_proj(x2d, w["norm_mix"][0:1], w["w_in"], tm)
    if past_k is None:
        attn = _sb_prompt(q, kb, vb, batch, seq)
    else:
        past = past_k.shape[1]
        attn = _sb_sample(q, kb, vb, past_k.reshape(batch, past, SB_WIDTH),
                          past_v.reshape(batch, past, SB_WIDTH), batch, seq,
                          _pick(past, (512, 256, 128)))
    lru_out, h_last, conv_new = _lru(u, g, h0, conv0, w["conv_w"], w["conv_b"], w["gate_w"],
                                     w["rg_b"], w["ig_b"], w["lam"], batch, seq,
                                     _pick(seq, (256, 128, 64, 32, 16, 8)))
    x1 = _mix_ffn(attn, lru_out, x2d, w["w_out"], w["norm_ffn"][0:1], w["ffn_gate"][0],
                  w["ffn_up"][0], w["ffn_down"][0], tm)
    if seq >= 512:
        seg, nb = _pick(seq, (512, 256, 128)), 1
    else:
        seg, nb = seq, _pick(batch, (8, 4, 2, 1))
    y, pool_new = _pool_ffn(x1, pool0, start_pos, w["norm_mix"][1:2], w["pool_w"], w["pool_scale"],
                            w["norm_ffn"][1:2], w["ffn_gate"][1], w["ffn_up"][1], w["ffn_down"][1],
                            w["norm_final"], batch, seq, seg, nb)
    heads = (N_SB_HEADS, SB_HEAD_DIM)
    return (y.reshape(batch, seq, d), k.reshape(1, batch, seq, *heads), v.reshape(1, batch, seq, *heads),
            h_last[None], conv_new[None], pool_new[None])


def kernel(x_prompt, x_sample, cache_sb_k, cache_sb_v, state_lru_h, state_lru_conv, state_pool,
           hyb_w_in, hyb_conv_w, hyb_conv_b, hyb_rg_w, hyb_rg_b, hyb_ig_w, hyb_ig_b, hyb_lambda,
           hyb_w_out, pool_w, pool_scale, norm_mix, norm_ffn, ffn_gate, ffn_up, ffn_down, norm_final):
    assert norm_mix.shape[0] == 2 and hyb_w_in.shape[0] == 1 and pool_w.shape[0] == 1
    d = x_prompt.shape[-1]
    d_rnn = hyb_lambda.shape[-1]
    w = {
        "w_in": hyb_w_in[0].astype(BF16),
        "conv_w": hyb_conv_w[0],
        "conv_b": hyb_conv_b[0].reshape(1, d_rnn),
        "gate_w": _gate_pairs(hyb_rg_w[0], hyb_ig_w[0]),
        "rg_b": hyb_rg_b[0].reshape(1, d_rnn),
        "ig_b": hyb_ig_b[0].reshape(1, d_rnn),
        "lam": hyb_lambda[0].reshape(1, d_rnn),
        "w_out": hyb_w_out[0].astype(BF16),
        "pool_w": pool_w[0].astype(BF16),
        "pool_scale": pool_scale[0].reshape(1, d),
        "norm_mix": norm_mix,
        "norm_ffn": norm_ffn,
        "ffn_gate": ffn_gate.astype(BF16),
        "ffn_up": ffn_up.astype(BF16),
        "ffn_down": ffn_down.astype(BF16),
        "norm_final": norm_final.reshape(1, d),
    }
    b = x_prompt.shape[0]
    dt = x_prompt.dtype
    y_p, k_p, v_p, h_p, conv_p, pool_p = _trunk(
        x_prompt, None, None, jnp.zeros((b, d_rnn), dt), jnp.zeros((b, CONV_W - 1, d_rnn), dt),
        jnp.zeros((b, POOL_HIST - 1, d), dt), 0, w)
    y_s, k_s, v_s, h_s, conv_s, pool_s = _trunk(
        x_sample, cache_sb_k[0], cache_sb_v[0], state_lru_h[0], state_lru_conv[0], state_pool[0],
        cache_sb_k.shape[2], w)
    return (y_p, y_s, k_p, v_p, h_p, conv_p, pool_p, k_s, v_s, h_s, conv_s, pool_s)
```

```python
import functools

import jax
import jax.numpy as jnp
from jax import lax
from jax.experimental import pallas as pl
from jax.experimental.pallas import tpu as pltpu

F32 = jnp.float32
BF16 = jnp.bfloat16

N_SB_HEADS = 8
SB_HEAD_DIM = 64
SB_WIDTH = N_SB_HEADS * SB_HEAD_DIM
SB_SCALE = SB_HEAD_DIM ** -0.5
N_LRU_BLOCKS = 8
CONV_W = 4
LRU_C = 8.0
POOL_WINDOWS = (2, 4, 8, 16)
POOL_HIST = 16
EPS = 1e-6

LANES = 128
SUBLANES = 8
HEAD_PAIRS = SB_WIDTH // LANES
KEY_BLOCK = LANES
EXP_ZERO_BELOW = -104.0
VMEM_LIMIT_BYTES = 56 * 1024 * 1024


def _cparams(n_axes):
    return pltpu.CompilerParams(dimension_semantics=("arbitrary",) * n_axes,
                                vmem_limit_bytes=VMEM_LIMIT_BYTES)


def _resident(shape):
    zeros = (0,) * len(shape)
    return pl.BlockSpec(shape, lambda *_: zeros, pipeline_mode=pl.Buffered(1))


def _rms(x, gain):
    ms = jnp.mean(x * x, axis=-1, keepdims=True)
    return (x * lax.rsqrt(ms + EPS)) * gain


def _dot(a, b):
    return jnp.dot(a, b, preferred_element_type=F32)


def _dot_nt(a, b):
    return lax.dot_general(a, b, (((1,), (1,)), ((), ())), preferred_element_type=F32)


def _in_proj_kernel(x_ref, gain_ref, wq_ref, wkv_ref, wug_ref, q_ref, k_ref, v_ref, kb_ref, vb_ref,
                    u_ref, g_ref, *, keys_on_lanes):
    xb = _rms(x_ref[...], gain_ref[...]).astype(BF16)
    w = SB_WIDTH
    q_ref[...] = (_dot(xb, wq_ref[...]) * SB_SCALE).astype(BF16)
    if keys_on_lanes:
        for dst, dst_b, rows in ((k_ref, kb_ref, slice(0, w)), (v_ref, vb_ref, slice(w, 2 * w))):
            t = _dot_nt(wkv_ref[rows, :], xb)
            dst[...] = t
            for blk in range(dst_b.shape[0]):
                dst_b[blk] = t[:, blk * KEY_BLOCK:(blk + 1) * KEY_BLOCK].astype(BF16)
    else:
        for dst, dst_b, cols in ((k_ref, kb_ref, slice(0, w)), (v_ref, vb_ref, slice(w, 2 * w))):
            t = _dot(xb, wkv_ref[:, cols])
            dst[...] = t
            dst_b[...] = t.astype(BF16)
    d_rnn = u_ref.shape[-1]
    u_ref[...] = _dot(xb, wug_ref[:, 0:d_rnn])
    g_ref[...] = _dot(xb, wug_ref[:, d_rnn:2 * d_rnn])


def _in_proj(x2d, gain, wq, wkv, wug, tm, batch, seq, keys_on_lanes):
    rows, d = x2d.shape
    d_rnn = wug.shape[1] // 2
    row = lambda width: pl.BlockSpec((tm, width), lambda i: (i, 0))
    if keys_on_lanes:
        assert seq % tm == 0 and tm % KEY_BLOCK == 0
        tpb, kpt = seq // tm, tm // KEY_BLOCK
        kv = pl.BlockSpec((None, SB_WIDTH, tm), lambda i: (i // tpb, 0, i % tpb))
        kvb = pl.BlockSpec((None, kpt, SB_WIDTH, KEY_BLOCK), lambda i: (i // tpb, i % tpb, 0, 0))
        kv_shape = jax.ShapeDtypeStruct((batch, SB_WIDTH, seq), F32)
        kvb_shape = jax.ShapeDtypeStruct((batch, seq // KEY_BLOCK, SB_WIDTH, KEY_BLOCK), BF16)
    else:
        kv = kvb = row(SB_WIDTH)
        kv_shape = jax.ShapeDtypeStruct((rows, SB_WIDTH), F32)
        kvb_shape = jax.ShapeDtypeStruct((rows, SB_WIDTH), BF16)
    return pl.pallas_call(
        functools.partial(_in_proj_kernel, keys_on_lanes=keys_on_lanes),
        grid=(rows // tm,),
        in_specs=[row(d), _resident((1, d)), _resident(wq.shape), _resident(wkv.shape), _resident(wug.shape)],
        out_specs=[row(SB_WIDTH), kv, kv, kvb, kvb, row(d_rnn), row(d_rnn)],
        out_shape=[jax.ShapeDtypeStruct((rows, SB_WIDTH), BF16), kv_shape, kv_shape, kvb_shape, kvb_shape,
                   jax.ShapeDtypeStruct((rows, d_rnn), F32), jax.ShapeDtypeStruct((rows, d_rnn), F32)],
        compiler_params=_cparams(1),
        name="in_proj",
    )(x2d, gain, wq, wkv, wug)


def _cumsum_matrix():
    s = lax.broadcasted_iota(jnp.int32, (KEY_BLOCK, 2 * KEY_BLOCK), 0)
    j = lax.broadcasted_iota(jnp.int32, (KEY_BLOCK, 2 * KEY_BLOCK), 1)
    return ((s > j) | (j >= KEY_BLOCK)).astype(BF16)


def _sb_begin(q_ref, qq_ref, acc_ref, car_ref):
    tq = q_ref.shape[0]
    acc_ref[...] = jnp.zeros_like(acc_ref)
    car_ref[...] = jnp.zeros_like(car_ref)
    low_half = lax.broadcasted_iota(jnp.int32, (tq, LANES), 1) < SB_HEAD_DIM
    for p in range(HEAD_PAIRS):
        q2 = q_ref[:, p * LANES:(p + 1) * LANES]
        zero = jnp.zeros_like(q2)
        qq_ref[p, 0:tq, :] = jnp.where(low_half, q2, zero)
        qq_ref[p, tq:2 * tq, :] = jnp.where(low_half, zero, q2)


def _sb_tile(qq_ref, kblk, vblk, acc_ref, car_ref, *, diag, keys_on_lanes):
    tq = qq_ref.shape[1] // 2
    assert tq & (tq - 1) == 0
    cm = _cumsum_matrix()
    pair = lambda p: slice(p * LANES, (p + 1) * LANES)
    if keys_on_lanes:
        zs = [_dot(qq_ref[p], kblk[pair(p), :]) for p in range(HEAD_PAIRS)]
    else:
        zs = [_dot_nt(qq_ref[p], kblk[:, pair(p)]) for p in range(HEAD_PAIRS)]
    z = jnp.concatenate(zs, axis=0)
    soft = jnp.log(1.0 + jnp.exp(-jnp.abs(z)))
    log_beta = jnp.minimum(z, 0.0) - soft
    log_not = log_beta - z
    if diag:
        key = lax.broadcasted_iota(jnp.int32, z.shape, 1)
        query = lax.broadcasted_iota(jnp.int32, z.shape, 0) & (tq - 1)
        before = key < query
        log_not = jnp.where(before, log_not, 0.0)
    hi = log_not.astype(BF16)
    lo = (log_not - hi.astype(F32)).astype(BF16)
    sums = _dot(hi, cm) + _dot(lo, cm)
    car = car_ref[...]
    wgt = jnp.exp(log_beta + sums[:, :KEY_BLOCK] + car)
    if diag:
        wgt = jnp.where(before, wgt, 0.0)
    car_ref[...] = car + sums[:, KEY_BLOCK:]
    wb = wgt.astype(BF16)
    low_half = lax.broadcasted_iota(jnp.int32, (tq, LANES), 1) < SB_HEAD_DIM
    for p in range(HEAD_PAIRS):
        w2 = wb[2 * p * tq:2 * (p + 1) * tq, :]
        pv = _dot_nt(w2, vblk[pair(p), :]) if keys_on_lanes else _dot(w2, vblk[:, pair(p)])
        acc_ref[p] += jnp.where(low_half, pv[0:tq, :], pv[tq:2 * tq, :])


def _sb_all_faded(car_ref):
    return jnp.max(car_ref[...]) < EXP_ZERO_BELOW


def _sb_write(o_ref, acc_ref):
    for p in range(HEAD_PAIRS):
        o_ref[:, p * LANES:(p + 1) * LANES] = acc_ref[p].astype(o_ref.dtype)


def _sb_scratch(tq):
    return [pltpu.VMEM((HEAD_PAIRS, 2 * tq, LANES), BF16),
            pltpu.VMEM((HEAD_PAIRS, tq, LANES), F32),
            pltpu.VMEM((N_SB_HEADS * tq, LANES), F32)]


def _sb_prompt_kernel(q_ref, k_ref, v_ref, o_ref, qq_ref, acc_ref, car_ref):
    i = pl.program_id(1)
    _sb_begin(q_ref, qq_ref, acc_ref, car_ref)
    _sb_tile(qq_ref, k_ref[i], v_ref[i], acc_ref, car_ref, diag=True, keys_on_lanes=True)

    def more(state):
        blk, faded = state
        return jnp.logical_and(blk >= 0, jnp.logical_not(faded))

    def step(state):
        blk, _ = state
        _sb_tile(qq_ref, k_ref[blk], v_ref[blk], acc_ref, car_ref, diag=False, keys_on_lanes=True)
        return blk - 1, _sb_all_faded(car_ref)

    lax.while_loop(more, step, (i - 1, _sb_all_faded(car_ref)))
    _sb_write(o_ref, acc_ref)


def _sb_prompt(q, kb, vb):
    batch, nq = kb.shape[0], kb.shape[1]
    qspec = pl.BlockSpec((KEY_BLOCK, SB_WIDTH), lambda b, i: (b * nq + i, 0))
    kvspec = pl.BlockSpec((None, nq, SB_WIDTH, KEY_BLOCK), lambda b, i: (b, 0, 0, 0))
    return pl.pallas_call(
        _sb_prompt_kernel,
        grid=(batch, nq),
        in_specs=[qspec, kvspec, kvspec],
        out_specs=qspec,
        out_shape=jax.ShapeDtypeStruct(q.shape, BF16),
        scratch_shapes=_sb_scratch(KEY_BLOCK),
        compiler_params=_cparams(2),
        name="sb_prompt",
    )(q, kb, vb)


def _sb_sample_kernel(q_ref, kn_ref, vn_ref, pk_hbm, pv_hbm, o_ref, qq_ref, acc_ref, car_ref,
                      kbuf, vbuf, sem):
    b = pl.program_id(0)
    nblk = pk_hbm.shape[2] // KEY_BLOCK

    def copies(blk):
        slot = blk & 1
        lanes = pl.ds(pl.multiple_of(blk * KEY_BLOCK, KEY_BLOCK), KEY_BLOCK)
        return (pltpu.make_async_copy(pk_hbm.at[b, :, lanes], kbuf.at[slot], sem.at[0, slot]),
                pltpu.make_async_copy(pv_hbm.at[b, :, lanes], vbuf.at[slot], sem.at[1, slot]))

    def start(blk):
        for c in copies(blk):
            c.start()

    def wait(blk):
        for c in copies(blk):
            c.wait()

    start(nblk - 1)
    _sb_begin(q_ref, qq_ref, acc_ref, car_ref)
    _sb_tile(qq_ref, kn_ref[...], vn_ref[...], acc_ref, car_ref, diag=True, keys_on_lanes=False)

    def more(state):
        blk, faded = state
        return jnp.logical_and(blk >= 0, jnp.logical_not(faded))

    def step(state):
        blk, _ = state
        wait(blk)

        @pl.when(blk >= 1)
        def _():
            start(blk - 1)

        slot = blk & 1
        _sb_tile(qq_ref, kbuf[slot].astype(BF16), vbuf[slot].astype(BF16), acc_ref, car_ref,
                 diag=False, keys_on_lanes=True)
        return blk - 1, _sb_all_faded(car_ref)

    blk, _ = lax.while_loop(more, step, (nblk - 1, _sb_all_faded(car_ref)))

    @pl.when(blk >= 0)
    def _():
        wait(blk)

    _sb_write(o_ref, acc_ref)


def _sb_sample(q, kb, vb, past_k, past_v, batch, seq):
    past = past_k.shape[2]
    assert seq <= KEY_BLOCK and seq % 16 == 0 and past % KEY_BLOCK == 0 and past > 0
    pad = ((0, 0), (0, KEY_BLOCK - seq), (0, 0))
    kn = jnp.pad(kb.reshape(batch, seq, SB_WIDTH), pad).reshape(batch * KEY_BLOCK, SB_WIDTH)
    vn = jnp.pad(vb.reshape(batch, seq, SB_WIDTH), pad).reshape(batch * KEY_BLOCK, SB_WIDTH)
    qspec = pl.BlockSpec((seq, SB_WIDTH), lambda b: (b, 0))
    nspec = pl.BlockSpec((KEY_BLOCK, SB_WIDTH), lambda b: (b, 0))
    hbm = pl.BlockSpec(memory_space=pl.ANY)
    return pl.pallas_call(
        _sb_sample_kernel,
        grid=(batch,),
        in_specs=[qspec, nspec, nspec, hbm, hbm],
        out_specs=qspec,
        out_shape=jax.ShapeDtypeStruct(q.shape, BF16),
        scratch_shapes=_sb_scratch(seq) + [pltpu.VMEM((2, SB_WIDTH, KEY_BLOCK), F32),
                                           pltpu.VMEM((2, SB_WIDTH, KEY_BLOCK), F32),
                                           pltpu.SemaphoreType.DMA((2, 2))],
        compiler_params=_cparams(1),
        name="sb_sample",
    )(q, kn, vn, past_k, past_v)


def _shift_rows(x, d, fill):
    row = lax.broadcasted_iota(jnp.int32, x.shape, 0)
    return jnp.where((row % SUBLANES) >= d, pltpu.roll(x, d, 0), fill)


def _lru_kernel(u_ref, g_ref, h0_ref, c0_ref, cw_ref, cb_ref, gw_ref, rb_ref, ib_ref, lam_ref,
                o_ref, hl_ref, cn_ref, h_ref, tail_ref):
    i = pl.program_id(1)
    tc, d_rnn = u_ref.shape

    @pl.when(i == 0)
    def _():
        h_ref[...] = jnp.broadcast_to(h0_ref[...], h_ref.shape)
        tail_ref[...] = c0_ref[...]

    u = u_ref[...]
    ext = jnp.concatenate([tail_ref[...], u], axis=0)
    tail_ref[...] = ext[tc:, :]
    uc = cb_ref[...] + u * cw_ref[CONV_W - 1:CONV_W, :]
    for back in range(1, CONV_W):
        uc = uc + ext[SUBLANES - back:SUBLANES - back + tc, :] * cw_ref[CONV_W - 1 - back:CONV_W - back, :]

    ucb = uc.astype(BF16)
    gates = [_dot(ucb[:, p * LANES:(p + 1) * LANES], gw_ref[p]) for p in range(d_rnn // LANES)]
    r = jax.nn.sigmoid(jnp.concatenate([gt[:, :LANES] for gt in gates], axis=1) + rb_ref[...])
    gi = jax.nn.sigmoid(jnp.concatenate([gt[:, LANES:] for gt in gates], axis=1) + ib_ref[...])
    lam = lam_ref[...]
    softplus_neg = jnp.maximum(-lam, 0.0) + jnp.log1p(jnp.exp(-jnp.abs(lam)))
    log_a = (-LRU_C * r) * softplus_neg
    a = jnp.exp(log_a)
    th = jnp.tanh(log_a)
    x_in = jnp.sqrt((-2.0 * th) / (1.0 - th)) * (gi * uc)

    d = 1
    while d < SUBLANES:
        x_in = a * _shift_rows(x_in, d, 0.0) + x_in
        a = a * _shift_rows(a, d, 1.0)
        d *= 2
    h = h_ref[0:1, :]
    rows = []
    for grp in range(tc // SUBLANES):
        sl = slice(grp * SUBLANES, (grp + 1) * SUBLANES)
        hg = x_in[sl, :] + a[sl, :] * h
        rows.append(hg)
        h = hg[SUBLANES - 1:SUBLANES, :]
    h_ref[...] = jnp.broadcast_to(h, h_ref.shape)
    h_all = jnp.concatenate(rows, axis=0)

    g = g_ref[...]
    gelu = 0.5 * g * (1.0 + jnp.tanh(0.7978845608028654 * (g + 0.044715 * (g * g * g))))
    o_ref[...] = (h_all * gelu).astype(o_ref.dtype)

    @pl.when(i == pl.num_programs(1) - 1)
    def _():
        hl_ref[...] = h
        cn_ref[...] = ext[tc:, :]


def _lru(u, g, h0, conv0, conv_w, conv_b, gate_w, rg_b, ig_b, lam, batch, seq, tc):
    d_rnn = u.shape[1]
    assert seq % tc == 0 and tc % SUBLANES == 0 and seq >= CONV_W - 1
    nt = seq // tc
    conv0p = jnp.pad(conv0, ((0, 0), (SUBLANES - (CONV_W - 1), 0), (0, 0)))
    row = pl.BlockSpec((tc, d_rnn), lambda b, i: (b * nt + i, 0))
    per_batch = lambda n: pl.BlockSpec((None, n, d_rnn), lambda b, i: (b, 0, 0))
    vec = _resident((1, d_rnn))
    out, h_last, conv_new = pl.pallas_call(
        _lru_kernel,
        grid=(batch, nt),
        in_specs=[row, row, per_batch(1), per_batch(SUBLANES), _resident(conv_w.shape), vec,
                  _resident(gate_w.shape), vec, vec, vec],
        out_specs=[row, per_batch(1), per_batch(SUBLANES)],
        out_shape=[jax.ShapeDtypeStruct(u.shape, BF16),
                   jax.ShapeDtypeStruct((batch, 1, d_rnn), F32),
                   jax.ShapeDtypeStruct((batch, SUBLANES, d_rnn), F32)],
        scratch_shapes=[pltpu.VMEM((SUBLANES, d_rnn), F32), pltpu.VMEM((SUBLANES, d_rnn), F32)],
        compiler_params=_cparams(2),
        name="rg_lru",
    )(u, g, h0.reshape(batch, 1, d_rnn), conv0p, conv_w, conv_b, gate_w, rg_b, ig_b, lam)
    return out, h_last.reshape(batch, d_rnn), conv_new[:, SUBLANES - (CONV_W - 1):, :]


def _ffn(xb, wg_ref, wu_ref, wd_ref, chunk):
    d_ff = wg_ref.shape[1]
    acc = None
    for c in range(0, d_ff, chunk):
        gate = _dot(xb, wg_ref[:, c:c + chunk])
        up = _dot(xb, wu_ref[:, c:c + chunk])
        hid = (gate * jax.nn.sigmoid(gate) * up).astype(BF16)
        part = _dot(hid, wd_ref[c:c + chunk, :])
        acc = part if acc is None else acc + part
    return acc


def _ffn_chunk(d_ff):
    for chunk in (512, 256, 128):
        if d_ff % chunk == 0:
            return chunk
    return d_ff


def _mix_ffn_kernel(a_ref, l_ref, x_ref, wo_ref, gain_ref, wg_ref, wu_ref, wd_ref, o_ref, *, chunk):
    sbw = a_ref.shape[1]
    y = _dot(a_ref[...], wo_ref[0:sbw, :]) + _dot(l_ref[...], wo_ref[sbw:, :])
    x1 = x_ref[...] + y
    xb = _rms(x1, gain_ref[...]).astype(BF16)
    o_ref[...] = x1 + _ffn(xb, wg_ref, wu_ref, wd_ref, chunk)


def _mix_ffn(attn, lru_out, x2d, w_out, gain, wg, wu, wd, tm):
    rows, d = x2d.shape
    row = lambda width: pl.BlockSpec((tm, width), lambda i: (i, 0))
    return pl.pallas_call(
        functools.partial(_mix_ffn_kernel, chunk=_ffn_chunk(wg.shape[1])),
        grid=(rows // tm,),
        in_specs=[row(attn.shape[1]), row(lru_out.shape[1]), row(d), _resident(w_out.shape),
                  _resident((1, d)), _resident(wg.shape), _resident(wu.shape), _resident(wd.shape)],
        out_specs=row(d),
        out_shape=jax.ShapeDtypeStruct((rows, d), F32),
        compiler_params=_cparams(1),
        name="mix_ffn",
    )(attn, lru_out, x2d, w_out, gain, wg, wu, wd)


def _pool_delta(ext, n, pos0):
    d = ext.shape[1]
    gw = d // len(POOL_WINDOWS)
    pos = pos0 + lax.broadcasted_iota(jnp.int32, (n, 1), 0)
    cur = ext[POOL_HIST:, :]
    sums, s, col0 = [], ext, 0
    for w in POOL_WINDOWS:
        half = w // 2
        s = s[half:, col0:] + s[:-half, col0:]
        first = POOL_HIST - (w - 1)
        sums.append(s[first:first + n, :gw])
        col0 = gw
    outs = []
    for g, w in enumerate(POOL_WINDOWS):
        inv = 1.0 / jnp.minimum(w, pos + 1).astype(F32)
        outs.append(sums[g] * inv - cur[:, g * gw:(g + 1) * gw])
    return outs


def _pool_ffn_kernel(x_ref, buf_ref, gm_ref, pw_ref, ps_ref, gf_ref, wg_ref, wu_ref, wd_ref,
                     gl_ref, o_ref, pn_ref, hist_ref, d_ref, *, chunk, segs_per_batch, start_pos):
    nb, seg, d = x_ref.shape
    gw = d // len(POOL_WINDOWS)
    i = pl.program_id(0)
    x = x_ref[...].reshape(nb * seg, d)
    xn = _rms(x, gm_ref[...])
    if segs_per_batch == 1:
        seg_idx = 0
    else:
        seg_idx = i % segs_per_batch

        @pl.when(seg_idx == 0)
        def _():
            hist_ref[...] = buf_ref[0]

    for b in range(nb):
        hist = buf_ref[b] if segs_per_batch == 1 else hist_ref[...]
        ext = jnp.concatenate([hist, xn[b * seg:(b + 1) * seg, :]], axis=0)
        for g, dg in enumerate(_pool_delta(ext, seg, start_pos + seg_idx * seg)):
            d_ref[b * seg:(b + 1) * seg, g * gw:(g + 1) * gw] = dg.astype(BF16)
        tail = ext[seg:, :]
        if segs_per_batch == 1:
            pn_ref[b] = tail
        else:
            hist_ref[...] = tail

            @pl.when(seg_idx == segs_per_batch - 1)
            def _():
                pn_ref[0] = tail

    y = jnp.concatenate([_dot(d_ref[:, g * gw:(g + 1) * gw], pw_ref[g])
                         for g in range(len(POOL_WINDOWS))], axis=1)
    x1 = x + y * ps_ref[...]
    xb = _rms(x1, gf_ref[...]).astype(BF16)
    x2 = x1 + _ffn(xb, wg_ref, wu_ref, wd_ref, chunk)
    o_ref[...] = _rms(x2, gl_ref[...]).reshape(nb, seg, d)


def _pool_ffn(x2d, pool0, start_pos, gain_mix, pool_w, pool_scale, gain_ffn, wg, wu, wd,
              gain_final, batch, seq, seg, nb):
    d = x2d.shape[1]
    spb = seq // seg
    assert seq % seg == 0 and seg % SUBLANES == 0 and (nb == 1 or spb == 1) and batch % nb == 0
    nseg = batch * spb
    buf = jnp.pad(pool0, ((0, 0), (1, 0), (0, 0)))
    xspec = pl.BlockSpec((nb, seg, d), lambda i: (i, 0, 0))
    bspec = pl.BlockSpec((nb, POOL_HIST, d), (lambda i: (i, 0, 0)) if spb == 1 else (lambda i: (i // spb, 0, 0)))
    vec = _resident((1, d))
    y, pool_new = pl.pallas_call(
        functools.partial(_pool_ffn_kernel, chunk=_ffn_chunk(wg.shape[1]), segs_per_batch=spb,
                          start_pos=start_pos),
        grid=(nseg // nb,),
        in_specs=[xspec, bspec, vec, _resident(pool_w.shape), vec, vec, _resident(wg.shape),
                  _resident(wu.shape), _resident(wd.shape), vec],
        out_specs=[xspec, bspec],
        out_shape=[jax.ShapeDtypeStruct((nseg, seg, d), F32),
                   jax.ShapeDtypeStruct((batch, POOL_HIST, d), F32)],
        scratch_shapes=[pltpu.VMEM((POOL_HIST, d), F32), pltpu.VMEM((nb * seg, d), BF16)],
        compiler_params=_cparams(1),
        name="pool_ffn",
    )(x2d.reshape(nseg, seg, d), buf, gain_mix, pool_w, pool_scale, gain_ffn, wg, wu, wd, gain_final)
    return y.reshape(batch * seq, d), pool_new[:, 1:, :]


def _pick(n, candidates):
    for c in candidates:
        if n % c == 0:
            return c
    return n


def _gate_pairs(rg_w, ig_w):
    nblk, c, _ = rg_w.shape
    z = jnp.zeros((c, c), rg_w.dtype)

    def pair(w, p):
        return jnp.block([[w[2 * p], z], [z, w[2 * p + 1]]])

    return jnp.stack([jnp.concatenate([pair(rg_w, p), pair(ig_w, p)], axis=1)
                      for p in range(nblk // 2)]).astype(BF16)


def _trunk(x, past_k, past_v, h0, conv0, pool0, start_pos, w):
    batch, seq, d = x.shape
    rows = batch * seq
    x2d = x.reshape(rows, d)
    tm = _pick(rows, (512, 256, 128, 64, 32, 16, 8))
    fresh = past_k is None

    q, k, v, kb, vb, u, g = _in_proj(x2d, w["norm_mix"][0:1], w["wq"], w["wkv_t"] if fresh else w["wkv"],
                                     w["wug"], tm, batch, seq, keys_on_lanes=fresh)
    if fresh:
        attn = _sb_prompt(q, kb, vb)
    else:
        attn = _sb_sample(q, kb, vb, past_k, past_v, batch, seq)
    lru_out, h_last, conv_new = _lru(u, g, h0, conv0, w["conv_w"], w["conv_b"], w["gate_w"],
                                     w["rg_b"], w["ig_b"], w["lam"], batch, seq,
                                     _pick(seq, (256, 128, 64, 32, 16, 8)))
    x1 = _mix_ffn(attn, lru_out, x2d, w["w_out"], w["norm_ffn"][0:1], w["ffn_gate"][0],
                  w["ffn_up"][0], w["ffn_down"][0], tm)
    if seq >= 512:
        seg, nb = _pick(seq, (512, 256, 128)), 1
    else:
        seg, nb = seq, _pick(batch, (8, 4, 2, 1))
    y, pool_new = _pool_ffn(x1, pool0, start_pos, w["norm_mix"][1:2], w["pool_w"], w["pool_scale"],
                            w["norm_ffn"][1:2], w["ffn_gate"][1], w["ffn_up"][1], w["ffn_down"][1],
                            w["norm_final"], batch, seq, seg, nb)
    heads = (N_SB_HEADS, SB_HEAD_DIM)
    if fresh:
        k, v = (t.reshape(1, batch, *heads, seq).transpose(0, 1, 4, 2, 3) for t in (k, v))
    else:
        k, v = (t.reshape(1, batch, seq, *heads) for t in (k, v))
    return (y.reshape(batch, seq, d), k, v, h_last[None], conv_new[None], pool_new[None])


def kernel(x_prompt, x_sample, cache_sb_k, cache_sb_v, state_lru_h, state_lru_conv, state_pool,
           hyb_w_in, hyb_conv_w, hyb_conv_b, hyb_rg_w, hyb_rg_b, hyb_ig_w, hyb_ig_b, hyb_lambda,
           hyb_w_out, pool_w, pool_scale, norm_mix, norm_ffn, ffn_gate, ffn_up, ffn_down, norm_final):
    assert norm_mix.shape[0] == 2 and hyb_w_in.shape[0] == 1 and pool_w.shape[0] == 1
    d = x_prompt.shape[-1]
    d_rnn = hyb_lambda.shape[-1]
    w_in = hyb_w_in[0].astype(BF16)
    w = {
        "wq": w_in[:, 0:SB_WIDTH],
        "wkv": w_in[:, SB_WIDTH:3 * SB_WIDTH],
        "wkv_t": w_in[:, SB_WIDTH:3 * SB_WIDTH].T,
        "wug": w_in[:, 3 * SB_WIDTH:],
        "conv_w": hyb_conv_w[0],
        "conv_b": hyb_conv_b[0].reshape(1, d_rnn),
        "gate_w": _gate_pairs(hyb_rg_w[0], hyb_ig_w[0]),
        "rg_b": hyb_rg_b[0].reshape(1, d_rnn),
        "ig_b": hyb_ig_b[0].reshape(1, d_rnn),
        "lam": hyb_lambda[0].reshape(1, d_rnn),
        "w_out": hyb_w_out[0].astype(BF16),
        "pool_w": pool_w[0].astype(BF16),
        "pool_scale": pool_scale[0].reshape(1, d),
        "norm_mix": norm_mix,
        "norm_ffn": norm_ffn,
        "ffn_gate": ffn_gate.astype(BF16),
        "ffn_up": ffn_up.astype(BF16),
        "ffn_down": ffn_down.astype(BF16),
        "norm_final": norm_final.reshape(1, d),
    }
    b = x_prompt.shape[0]
    dt = x_prompt.dtype
    y_p, k_p, v_p, h_p, conv_p, pool_p = _trunk(
        x_prompt, None, None, jnp.zeros((b, d_rnn), dt), jnp.zeros((b, CONV_W - 1, d_rnn), dt),
        jnp.zeros((b, POOL_HIST - 1, d), dt), 0, w)
    db, past = cache_sb_k.shape[1], cache_sb_k.shape[2]
    past_k, past_v = (c[0].transpose(0, 2, 3, 1).reshape(db, SB_WIDTH, past) for c in (cache_sb_k, cache_sb_v))
    y_s, k_s, v_s, h_s, conv_s, pool_s = _trunk(
        x_sample, past_k, past_v, state_lru_h[0], state_lru_conv[0], state_pool[0], past, w)
    return (y_p, y_s, k_p, v_p, h_p, conv_p, pool_p, k_s, v_s, h_s, conv_s, pool_s)
```

```python
import functools

import jax
import jax.numpy as jnp
from jax import lax
from jax.experimental import pallas as pl
from jax.experimental.pallas import tpu as pltpu

F32 = jnp.float32
BF16 = jnp.bfloat16

N_SB_HEADS = 8
SB_HEAD_DIM = 64
SB_WIDTH = N_SB_HEADS * SB_HEAD_DIM
SB_SCALE = SB_HEAD_DIM ** -0.5
N_LRU_BLOCKS = 8
CONV_W = 4
LRU_C = 8.0
POOL_WINDOWS = (2, 4, 8, 16)
POOL_HIST = 16
EPS = 1e-6

LANES = 128
SUBLANES = 8
HEAD_PAIRS = SB_WIDTH // LANES
KEY_BLOCK = LANES
EXP_ZERO_BELOW = -104.0
VMEM_LIMIT_BYTES = 56 * 1024 * 1024


def _cparams(n_axes):
    return pltpu.CompilerParams(dimension_semantics=("arbitrary",) * n_axes,
                                vmem_limit_bytes=VMEM_LIMIT_BYTES)


def _resident(shape):
    zeros = (0,) * len(shape)
    return pl.BlockSpec(shape, lambda *_: zeros, pipeline_mode=pl.Buffered(1))


def _rms(x, gain):
    ms = jnp.mean(x * x, axis=-1, keepdims=True)
    return (x * lax.rsqrt(ms + EPS)) * gain


def _dot(a, b):
    return jnp.dot(a, b, preferred_element_type=F32)


def _dot_nt(a, b):
    return lax.dot_general(a, b, (((1,), (1,)), ((), ())), preferred_element_type=F32)


def _in_proj_kernel(x_ref, gain_ref, wq_ref, wkv_ref, wug_ref, q_ref, k_ref, v_ref, kb_ref, vb_ref,
                    u_ref, g_ref, *, keys_on_lanes):
    xb = _rms(x_ref[...], gain_ref[...]).astype(BF16)
    w = SB_WIDTH
    q_ref[...] = (_dot(xb, wq_ref[...]) * SB_SCALE).astype(BF16)
    if keys_on_lanes:
        for dst, dst_b, rows in ((k_ref, kb_ref, slice(0, w)), (v_ref, vb_ref, slice(w, 2 * w))):
            t = _dot_nt(wkv_ref[rows, :], xb)
            dst[...] = t
            for blk in range(dst_b.shape[0]):
                dst_b[blk] = t[:, blk * KEY_BLOCK:(blk + 1) * KEY_BLOCK].astype(BF16)
    else:
        for dst, dst_b, cols in ((k_ref, kb_ref, slice(0, w)), (v_ref, vb_ref, slice(w, 2 * w))):
            t = _dot(xb, wkv_ref[:, cols])
            dst[...] = t
            dst_b[...] = t.astype(BF16)
    d_rnn = u_ref.shape[-1]
    u_ref[...] = _dot(xb, wug_ref[:, 0:d_rnn])
    g_ref[...] = _dot(xb, wug_ref[:, d_rnn:2 * d_rnn])


def _in_proj(x2d, gain, wq, wkv, wug, tm, batch, seq, keys_on_lanes):
    rows, d = x2d.shape
    d_rnn = wug.shape[1] // 2
    row = lambda width: pl.BlockSpec((tm, width), lambda i: (i, 0))
    if keys_on_lanes:
        assert seq % tm == 0 and tm % KEY_BLOCK == 0
        tpb, kpt = seq // tm, tm // KEY_BLOCK
        kv = pl.BlockSpec((None, SB_WIDTH, tm), lambda i: (i // tpb, 0, i % tpb))
        kvb = pl.BlockSpec((None, kpt, SB_WIDTH, KEY_BLOCK), lambda i: (i // tpb, i % tpb, 0, 0))
        kv_shape = jax.ShapeDtypeStruct((batch, SB_WIDTH, seq), F32)
        kvb_shape = jax.ShapeDtypeStruct((batch, seq // KEY_BLOCK, SB_WIDTH, KEY_BLOCK), BF16)
    else:
        kv = kvb = row(SB_WIDTH)
        kv_shape = jax.ShapeDtypeStruct((rows, SB_WIDTH), F32)
        kvb_shape = jax.ShapeDtypeStruct((rows, SB_WIDTH), BF16)
    return pl.pallas_call(
        functools.partial(_in_proj_kernel, keys_on_lanes=keys_on_lanes),
        grid=(rows // tm,),
        in_specs=[row(d), _resident((1, d)), _resident(wq.shape), _resident(wkv.shape), _resident(wug.shape)],
        out_specs=[row(SB_WIDTH), kv, kv, kvb, kvb, row(d_rnn), row(d_rnn)],
        out_shape=[jax.ShapeDtypeStruct((rows, SB_WIDTH), BF16), kv_shape, kv_shape, kvb_shape, kvb_shape,
                   jax.ShapeDtypeStruct((rows, d_rnn), F32), jax.ShapeDtypeStruct((rows, d_rnn), F32)],
        compiler_params=_cparams(1),
        name="in_proj",
    )(x2d, gain, wq, wkv, wug)


def _cumsum_matrix():
    s = lax.broadcasted_iota(jnp.int32, (KEY_BLOCK, 2 * KEY_BLOCK), 0)
    j = lax.broadcasted_iota(jnp.int32, (KEY_BLOCK, 2 * KEY_BLOCK), 1)
    return ((s > j) | (j >= KEY_BLOCK)).astype(BF16)


def _sb_begin(q_ref, qq_ref, acc_ref, car_ref):
    ns, tq, _ = q_ref.shape
    acc_ref[...] = jnp.zeros_like(acc_ref)
    car_ref[...] = jnp.zeros_like(car_ref)
    low_half = lax.broadcasted_iota(jnp.int32, (tq, LANES), 1) < SB_HEAD_DIM
    for s in range(ns):
        for p in range(HEAD_PAIRS):
            q2 = q_ref[s, :, p * LANES:(p + 1) * LANES]
            zero = jnp.zeros_like(q2)
            qq_ref[s, p, 0:tq, :] = jnp.where(low_half, q2, zero)
            qq_ref[s, p, tq:2 * tq, :] = jnp.where(low_half, zero, q2)


def _sb_tile(qq_ref, kblks, vblks, acc_ref, car_ref, *, diag, keys_on_lanes):
    ns, _, tq2, _ = qq_ref.shape
    tq = tq2 // 2
    assert tq & (tq - 1) == 0
    cm = _cumsum_matrix()
    cm2 = jnp.concatenate([cm, cm], axis=0)
    pair = lambda p: slice(p * LANES, (p + 1) * LANES)
    zs = []
    for s in range(ns):
        for p in range(HEAD_PAIRS):
            if keys_on_lanes:
                zs.append(_dot(qq_ref[s, p], kblks[s][pair(p), :]))
            else:
                zs.append(_dot_nt(qq_ref[s, p], kblks[s][:, pair(p)]))
    z = jnp.concatenate(zs, axis=0)
    soft = jnp.log(1.0 + jnp.exp(-jnp.abs(z)))
    log_beta = jnp.minimum(z, 0.0) - soft
    log_not = log_beta - z
    if diag:
        key = lax.broadcasted_iota(jnp.int32, z.shape, 1)
        query = lax.broadcasted_iota(jnp.int32, z.shape, 0) & (tq - 1)
        before = key < query
        log_not = jnp.where(before, log_not, 0.0)
    hi = log_not.astype(BF16)
    lo = (log_not - hi.astype(F32)).astype(BF16)
    sums = _dot(jnp.concatenate([hi, lo], axis=1), cm2)
    car = car_ref[...]
    wgt = jnp.exp(log_beta + sums[:, :KEY_BLOCK] + car)
    if diag:
        wgt = jnp.where(before, wgt, 0.0)
    car_ref[...] = car + sums[:, KEY_BLOCK:]
    wb = wgt.astype(BF16)
    low_half = lax.broadcasted_iota(jnp.int32, (tq, LANES), 1) < SB_HEAD_DIM
    for s in range(ns):
        for p in range(HEAD_PAIRS):
            r0 = (s * HEAD_PAIRS + p) * 2 * tq
            w2 = wb[r0:r0 + 2 * tq, :]
            pv = _dot_nt(w2, vblks[s][pair(p), :]) if keys_on_lanes else _dot(w2, vblks[s][:, pair(p)])
            acc_ref[s, p] += jnp.where(low_half, pv[0:tq, :], pv[tq:2 * tq, :])


def _sb_all_faded(car_ref):
    return jnp.max(car_ref[...]) < EXP_ZERO_BELOW


def _sb_write(o_ref, acc_ref):
    for s in range(acc_ref.shape[0]):
        for p in range(HEAD_PAIRS):
            o_ref[s, :, p * LANES:(p + 1) * LANES] = acc_ref[s, p].astype(o_ref.dtype)


def _sb_scratch(ns, tq):
    return [pltpu.VMEM((ns, HEAD_PAIRS, 2 * tq, LANES), BF16),
            pltpu.VMEM((ns, HEAD_PAIRS, tq, LANES), F32),
            pltpu.VMEM((ns * N_SB_HEADS * tq, LANES), F32)]


def _sb_prompt_kernel(q_ref, k_ref, v_ref, o_ref, qq_ref, acc_ref, car_ref):
    i = pl.program_id(1)
    ns = q_ref.shape[0]
    blocks = lambda ref, blk: [ref[s, blk] for s in range(ns)]
    _sb_begin(q_ref, qq_ref, acc_ref, car_ref)
    _sb_tile(qq_ref, blocks(k_ref, i), blocks(v_ref, i), acc_ref, car_ref, diag=True, keys_on_lanes=True)

    def more(state):
        blk, faded = state
        return jnp.logical_and(blk >= 0, jnp.logical_not(faded))

    def step(state):
        blk, _ = state
        _sb_tile(qq_ref, blocks(k_ref, blk), blocks(v_ref, blk), acc_ref, car_ref, diag=False,
                 keys_on_lanes=True)
        return blk - 1, _sb_all_faded(car_ref)

    lax.while_loop(more, step, (i - 1, _sb_all_faded(car_ref)))
    _sb_write(o_ref, acc_ref)


def _sb_prompt(q, kb, vb, ns):
    batch, nq = kb.shape[0], kb.shape[1]
    assert batch % ns == 0
    qspec = pl.BlockSpec((ns, KEY_BLOCK, SB_WIDTH), lambda b, i: (b, i, 0))
    kvspec = pl.BlockSpec((ns, nq, SB_WIDTH, KEY_BLOCK), lambda b, i: (b, 0, 0, 0))
    return pl.pallas_call(
        _sb_prompt_kernel,
        grid=(batch // ns, nq),
        in_specs=[qspec, kvspec, kvspec],
        out_specs=qspec,
        out_shape=jax.ShapeDtypeStruct(q.shape, BF16),
        scratch_shapes=_sb_scratch(ns, KEY_BLOCK),
        compiler_params=_cparams(2),
        name="sb_prompt",
    )(q, kb, vb)


def _sb_sample_kernel(q_ref, kn_ref, vn_ref, pk_hbm, pv_hbm, o_ref, qq_ref, acc_ref, car_ref,
                      kbuf, vbuf, sem):
    ns = q_ref.shape[0]
    b0 = pl.program_id(0) * ns
    nblk = pk_hbm.shape[2] // KEY_BLOCK

    def copies(blk):
        slot = blk & 1
        lanes = pl.ds(pl.multiple_of(blk * KEY_BLOCK, KEY_BLOCK), KEY_BLOCK)
        return [pltpu.make_async_copy(hbm.at[b0 + s, :, lanes], buf.at[slot, s], sem.at[a, slot, s])
                for a, (hbm, buf) in enumerate(((pk_hbm, kbuf), (pv_hbm, vbuf))) for s in range(ns)]

    def start(blk):
        for c in copies(blk):
            c.start()

    def wait(blk):
        for c in copies(blk):
            c.wait()

    start(nblk - 1)
    _sb_begin(q_ref, qq_ref, acc_ref, car_ref)
    _sb_tile(qq_ref, [kn_ref[s] for s in range(ns)], [vn_ref[s] for s in range(ns)], acc_ref, car_ref,
             diag=True, keys_on_lanes=False)

    def more(state):
        blk, faded = state
        return jnp.logical_and(blk >= 0, jnp.logical_not(faded))

    def step(state):
        blk, _ = state
        wait(blk)

        @pl.when(blk >= 1)
        def _():
            start(blk - 1)

        slot = blk & 1
        _sb_tile(qq_ref, [kbuf[slot, s].astype(BF16) for s in range(ns)],
                 [vbuf[slot, s].astype(BF16) for s in range(ns)], acc_ref, car_ref,
                 diag=False, keys_on_lanes=True)
        return blk - 1, _sb_all_faded(car_ref)

    blk, _ = lax.while_loop(more, step, (nblk - 1, _sb_all_faded(car_ref)))

    @pl.when(blk >= 0)
    def _():
        wait(blk)

    _sb_write(o_ref, acc_ref)


def _sb_sample(q, kb, vb, past_k, past_v, ns):
    batch, seq, _ = q.shape
    past = past_k.shape[2]
    assert seq <= KEY_BLOCK and seq % 16 == 0 and past % KEY_BLOCK == 0 and past > 0 and batch % ns == 0
    pad = ((0, 0), (0, KEY_BLOCK - seq), (0, 0))
    kn, vn = jnp.pad(kb, pad), jnp.pad(vb, pad)
    qspec = pl.BlockSpec((ns, seq, SB_WIDTH), lambda b: (b, 0, 0))
    nspec = pl.BlockSpec((ns, KEY_BLOCK, SB_WIDTH), lambda b: (b, 0, 0))
    hbm = pl.BlockSpec(memory_space=pl.ANY)
    return pl.pallas_call(
        _sb_sample_kernel,
        grid=(batch // ns,),
        in_specs=[qspec, nspec, nspec, hbm, hbm],
        out_specs=qspec,
        out_shape=jax.ShapeDtypeStruct(q.shape, BF16),
        scratch_shapes=_sb_scratch(ns, seq) + [pltpu.VMEM((2, ns, SB_WIDTH, KEY_BLOCK), F32),
                                               pltpu.VMEM((2, ns, SB_WIDTH, KEY_BLOCK), F32),
                                               pltpu.SemaphoreType.DMA((2, 2, ns))],
        compiler_params=_cparams(1),
        name="sb_sample",
    )(q, kn, vn, past_k, past_v)


def _lru_kernel(u_ref, g_ref, h0_ref, c0_ref, cw_ref, cb_ref, gw_ref, rb_ref, ib_ref, lam_ref,
                o_ref, hl_ref, cn_ref, h_ref, ext_ref, a_ref, x_ref, hseq_ref):
    i = pl.program_id(1)
    nb, tc, d = u_ref.shape
    n = nb * tc
    hist = SUBLANES

    @pl.when(i == 0)
    def _():
        h_ref[...] = h0_ref[...]
        ext_ref[:, 0:hist, :] = c0_ref[...]

    u3 = u_ref[...]
    ext_ref[:, hist:, :] = u3
    uc = cb_ref[...] + u3.reshape(n, d) * cw_ref[CONV_W - 1:CONV_W, :]
    for back in range(1, CONV_W):
        prev = ext_ref[:, hist - back:hist - back + tc, :].reshape(n, d)
        uc = uc + prev * cw_ref[CONV_W - 1 - back:CONV_W - back, :]
    tail = ext_ref[:, tc:tc + hist, :]
    ext_ref[:, 0:hist, :] = tail

    ucb = uc.astype(BF16)
    gates = [_dot(ucb[:, p * LANES:(p + 1) * LANES], gw_ref[p]) for p in range(d // LANES)]
    tr = jnp.tanh(jnp.concatenate([gt[:, :LANES] for gt in gates], axis=1) + rb_ref[...])
    ti = jnp.tanh(jnp.concatenate([gt[:, LANES:] for gt in gates], axis=1) + ib_ref[...])
    lam = lam_ref[...]
    softplus_neg = jnp.maximum(-lam, 0.0) + jnp.log1p(jnp.exp(-jnp.abs(lam)))
    log_a = (tr + 1.0) * ((-0.5 * LRU_C) * softplus_neg)
    a = jnp.exp(log_a)
    th = jnp.tanh(log_a)
    p = -2.0 * th
    mult = p * lax.rsqrt(jnp.maximum(p * (1.0 - th), jnp.finfo(F32).tiny))
    x_in = (mult * uc) * (0.5 * ti + 0.5)
    nlb = d // LANES
    pitch = a_ref.shape[1] // nb
    for c in range(nlb):
        for b in range(nb):
            dst = slice(b * pitch, b * pitch + tc)
            a_ref[c, dst, :] = a[b * tc:(b + 1) * tc, c * LANES:(c + 1) * LANES]
            x_ref[c, dst, :] = x_in[b * tc:(b + 1) * tc, c * LANES:(c + 1) * LANES]

    hs = [h_ref[:, c * LANES:(c + 1) * LANES] for c in range(nlb)]
    for t in range(tc):
        step = pl.ds(t, nb, stride=pitch)
        for c in range(nlb):
            hs[c] = a_ref[c, step, :] * hs[c] + x_ref[c, step, :]
            hseq_ref[c, step, :] = hs[c]
    h = jnp.concatenate(hs, axis=1)
    h_ref[...] = h
    h_all = jnp.concatenate(
        [jnp.concatenate([hseq_ref[c, b * pitch:b * pitch + tc, :] for b in range(nb)], axis=0)
         for c in range(nlb)], axis=1)

    g = g_ref[...].reshape(n, d)
    inner = g * (0.7978845608028654 + (0.7978845608028654 * 0.044715) * (g * g))
    o_ref[...] = ((h_all * g) * (0.5 * jnp.tanh(inner) + 0.5)).astype(o_ref.dtype).reshape(nb, tc, d)

    @pl.when(i == pl.num_programs(1) - 1)
    def _():
        hl_ref[...] = h
        cn_ref[...] = tail


def _lru(u, g, h0, conv0, conv_w, conv_b, gate_w, rg_b, ig_b, lam, tc):
    batch, seq, d_rnn = u.shape
    nb = SUBLANES
    assert batch % nb == 0 and seq % tc == 0 and tc % 16 == 0 and seq >= CONV_W - 1
    conv0p = jnp.pad(conv0, ((0, 0), (SUBLANES - (CONV_W - 1), 0), (0, 0)))
    chunk = pl.BlockSpec((nb, tc, d_rnn), lambda b, i: (b, i, 0))
    state = pl.BlockSpec((nb, d_rnn), lambda b, i: (b, 0))
    tails = pl.BlockSpec((nb, SUBLANES, d_rnn), lambda b, i: (b, 0, 0))
    vec = _resident((1, d_rnn))
    pitch = tc + SUBLANES if (tc // SUBLANES) % 2 == 0 else tc
    scan_buf = pltpu.VMEM((d_rnn // LANES, nb * pitch, LANES), F32)
    out, h_last, conv_new = pl.pallas_call(
        _lru_kernel,
        grid=(batch // nb, seq // tc),
        in_specs=[chunk, chunk, state, tails, _resident(conv_w.shape), vec,
                  _resident(gate_w.shape), vec, vec, vec],
        out_specs=[chunk, state, tails],
        out_shape=[jax.ShapeDtypeStruct(u.shape, BF16),
                   jax.ShapeDtypeStruct((batch, d_rnn), F32),
                   jax.ShapeDtypeStruct((batch, SUBLANES, d_rnn), F32)],
        scratch_shapes=[pltpu.VMEM((nb, d_rnn), F32),
                        pltpu.VMEM((nb, SUBLANES + tc, d_rnn), F32),
                        scan_buf, scan_buf, scan_buf],
        compiler_params=_cparams(2),
        name="rg_lru",
    )(u, g, h0, conv0p, conv_w, conv_b, gate_w, rg_b, ig_b, lam)
    return out, h_last, conv_new[:, SUBLANES - (CONV_W - 1):, :]


def _ffn(xb, wg_ref, wu_ref, wd_ref, chunk):
    d_ff = wg_ref.shape[1]
    acc = None
    for c in range(0, d_ff, chunk):
        gate = _dot(xb, wg_ref[:, c:c + chunk])
        up = _dot(xb, wu_ref[:, c:c + chunk])
        hid = (gate * jax.nn.sigmoid(gate) * up).astype(BF16)
        part = _dot(hid, wd_ref[c:c + chunk, :])
        acc = part if acc is None else acc + part
    return acc


def _ffn_chunk(d_ff):
    for chunk in (512, 256, 128):
        if d_ff % chunk == 0:
            return chunk
    return d_ff


def _mix_ffn_kernel(a_ref, l_ref, x_ref, wo_ref, gain_ref, wg_ref, wu_ref, wd_ref, o_ref, *, chunk):
    sbw = a_ref.shape[1]
    y = _dot(a_ref[...], wo_ref[0:sbw, :]) + _dot(l_ref[...], wo_ref[sbw:, :])
    x1 = x_ref[...] + y
    xb = _rms(x1, gain_ref[...]).astype(BF16)
    o_ref[...] = x1 + _ffn(xb, wg_ref, wu_ref, wd_ref, chunk)


def _mix_ffn(attn, lru_out, x2d, w_out, gain, wg, wu, wd, tm):
    rows, d = x2d.shape
    row = lambda width: pl.BlockSpec((tm, width), lambda i: (i, 0))
    return pl.pallas_call(
        functools.partial(_mix_ffn_kernel, chunk=_ffn_chunk(wg.shape[1])),
        grid=(rows // tm,),
        in_specs=[row(attn.shape[1]), row(lru_out.shape[1]), row(d), _resident(w_out.shape),
                  _resident((1, d)), _resident(wg.shape), _resident(wu.shape), _resident(wd.shape)],
        out_specs=row(d),
        out_shape=jax.ShapeDtypeStruct((rows, d), F32),
        compiler_params=_cparams(1),
        name="mix_ffn",
    )(attn, lru_out, x2d, w_out, gain, wg, wu, wd)


def _pool_delta(ext, n, pos0):
    d = ext.shape[1]
    gw = d // len(POOL_WINDOWS)
    pos = pos0 + lax.broadcasted_iota(jnp.int32, (n, 1), 0)
    cur = ext[POOL_HIST:, :]
    sums, s, col0 = [], ext, 0
    for w in POOL_WINDOWS:
        half = w // 2
        s = s[half:, col0:] + s[:-half, col0:]
        first = POOL_HIST - (w - 1)
        sums.append(s[first:first + n, :gw])
        col0 = gw
    outs = []
    for g, w in enumerate(POOL_WINDOWS):
        inv = 1.0 / jnp.minimum(w, pos + 1).astype(F32)
        outs.append(sums[g] * inv - cur[:, g * gw:(g + 1) * gw])
    return outs


def _pool_ffn_kernel(x_ref, buf_ref, gm_ref, pw_ref, ps_ref, gf_ref, wg_ref, wu_ref, wd_ref,
                     gl_ref, o_ref, pn_ref, hist_ref, d_ref, *, chunk, segs_per_batch, start_pos):
    nb, seg, d = x_ref.shape
    gw = d // len(POOL_WINDOWS)
    i = pl.program_id(0)
    x = x_ref[...].reshape(nb * seg, d)
    xn = _rms(x, gm_ref[...])
    if segs_per_batch == 1:
        seg_idx = 0
    else:
        seg_idx = i % segs_per_batch

        @pl.when(seg_idx == 0)
        def _():
            hist_ref[...] = buf_ref[0]

    for b in range(nb):
        hist = buf_ref[b] if segs_per_batch == 1 else hist_ref[...]
        ext = jnp.concatenate([hist, xn[b * seg:(b + 1) * seg, :]], axis=0)
        for g, dg in enumerate(_pool_delta(ext, seg, start_pos + seg_idx * seg)):
            d_ref[b * seg:(b + 1) * seg, g * gw:(g + 1) * gw] = dg.astype(BF16)
        tail = ext[seg:, :]
        if segs_per_batch == 1:
            pn_ref[b] = tail
        else:
            hist_ref[...] = tail

            @pl.when(seg_idx == segs_per_batch - 1)
            def _():
                pn_ref[0] = tail

    y = jnp.concatenate([_dot(d_ref[:, g * gw:(g + 1) * gw], pw_ref[g])
                         for g in range(len(POOL_WINDOWS))], axis=1)
    x1 = x + y * ps_ref[...]
    xb = _rms(x1, gf_ref[...]).astype(BF16)
    x2 = x1 + _ffn(xb, wg_ref, wu_ref, wd_ref, chunk)
    o_ref[...] = _rms(x2, gl_ref[...]).reshape(nb, seg, d)


def _pool_ffn(x2d, pool0, start_pos, gain_mix, pool_w, pool_scale, gain_ffn, wg, wu, wd,
              gain_final, batch, seq, seg, nb):
    d = x2d.shape[1]
    spb = seq // seg
    assert seq % seg == 0 and seg % SUBLANES == 0 and (nb == 1 or spb == 1) and batch % nb == 0
    nseg = batch * spb
    buf = jnp.pad(pool0, ((0, 0), (1, 0), (0, 0)))
    xspec = pl.BlockSpec((nb, seg, d), lambda i: (i, 0, 0))
    bspec = pl.BlockSpec((nb, POOL_HIST, d), (lambda i: (i, 0, 0)) if spb == 1 else (lambda i: (i // spb, 0, 0)))
    vec = _resident((1, d))
    y, pool_new = pl.pallas_call(
        functools.partial(_pool_ffn_kernel, chunk=_ffn_chunk(wg.shape[1]), segs_per_batch=spb,
                          start_pos=start_pos),
        grid=(nseg // nb,),
        in_specs=[xspec, bspec, vec, _resident(pool_w.shape), vec, vec, _resident(wg.shape),
                  _resident(wu.shape), _resident(wd.shape), vec],
        out_specs=[xspec, bspec],
        out_shape=[jax.ShapeDtypeStruct((nseg, seg, d), F32),
                   jax.ShapeDtypeStruct((batch, POOL_HIST, d), F32)],
        scratch_shapes=[pltpu.VMEM((POOL_HIST, d), F32), pltpu.VMEM((nb * seg, d), BF16)],
        compiler_params=_cparams(1),
        name="pool_ffn",
    )(x2d.reshape(nseg, seg, d), buf, gain_mix, pool_w, pool_scale, gain_ffn, wg, wu, wd, gain_final)
    return y.reshape(batch * seq, d), pool_new[:, 1:, :]


def _pick(n, candidates):
    for c in candidates:
        if n % c == 0:
            return c
    return n


def _gate_pairs(rg_w, ig_w):
    nblk, c, _ = rg_w.shape
    z = jnp.zeros((c, c), rg_w.dtype)

    def pair(w, p):
        return jnp.block([[w[2 * p], z], [z, w[2 * p + 1]]])

    return jnp.stack([jnp.concatenate([pair(rg_w, p), pair(ig_w, p)], axis=1)
                      for p in range(nblk // 2)]).astype(BF16)


def _trunk(x, past_k, past_v, h0, conv0, pool0, start_pos, w):
    batch, seq, d = x.shape
    rows = batch * seq
    x2d = x.reshape(rows, d)
    tm = _pick(rows, (512, 256, 128, 64, 32, 16, 8))
    fresh = past_k is None

    q, k, v, kb, vb, u, g = _in_proj(x2d, w["norm_mix"][0:1], w["wq"], w["wkv_t"] if fresh else w["wkv"],
                                     w["wug"], tm, batch, seq, keys_on_lanes=fresh)
    q3 = q.reshape(batch, seq, SB_WIDTH)
    if fresh:
        attn = _sb_prompt(q3, kb, vb, _pick(batch, (2, 1)))
    else:
        attn = _sb_sample(q3, kb.reshape(q3.shape), vb.reshape(q3.shape), past_k, past_v,
                          _pick(batch, (4, 2, 1)))
    attn = attn.reshape(rows, SB_WIDTH)
    d_rnn = u.shape[1]
    lru_out, h_last, conv_new = _lru(u.reshape(batch, seq, d_rnn), g.reshape(batch, seq, d_rnn), h0, conv0,
                                     w["conv_w"], w["conv_b"], w["gate_w"], w["rg_b"], w["ig_b"], w["lam"],
                                     _pick(seq, (128, 64, 32, 16)))
    lru_out = lru_out.reshape(rows, d_rnn)
    x1 = _mix_ffn(attn, lru_out, x2d, w["w_out"], w["norm_ffn"][0:1], w["ffn_gate"][0],
                  w["ffn_up"][0], w["ffn_down"][0], tm)
    if seq >= 512:
        seg, nb = _pick(seq, (512, 256, 128)), 1
    else:
        seg, nb = seq, _pick(batch, (8, 4, 2, 1))
    y, pool_new = _pool_ffn(x1, pool0, start_pos, w["norm_mix"][1:2], w["pool_w"], w["pool_scale"],
                            w["norm_ffn"][1:2], w["ffn_gate"][1], w["ffn_up"][1], w["ffn_down"][1],
                            w["norm_final"], batch, seq, seg, nb)
    heads = (N_SB_HEADS, SB_HEAD_DIM)
    if fresh:
        k, v = (t.reshape(1, batch, *heads, seq).transpose(0, 1, 4, 2, 3) for t in (k, v))
    else:
        k, v = (t.reshape(1, batch, seq, *heads) for t in (k, v))
    return (y.reshape(batch, seq, d), k, v, h_last[None], conv_new[None], pool_new[None])


def kernel(x_prompt, x_sample, cache_sb_k, cache_sb_v, state_lru_h, state_lru_conv, state_pool,
           hyb_w_in, hyb_conv_w, hyb_conv_b, hyb_rg_w, hyb_rg_b, hyb_ig_w, hyb_ig_b, hyb_lambda,
           hyb_w_out, pool_w, pool_scale, norm_mix, norm_ffn, ffn_gate, ffn_up, ffn_down, norm_final):
    assert norm_mix.shape[0] == 2 and hyb_w_in.shape[0] == 1 and pool_w.shape[0] == 1
    d = x_prompt.shape[-1]
    d_rnn = hyb_lambda.shape[-1]
    w_in = hyb_w_in[0].astype(BF16)
    w = {
        "wq": w_in[:, 0:SB_WIDTH],
        "wkv": w_in[:, SB_WIDTH:3 * SB_WIDTH],
        "wkv_t": w_in[:, SB_WIDTH:3 * SB_WIDTH].T,
        "wug": w_in[:, 3 * SB_WIDTH:],
        "conv_w": hyb_conv_w[0],
        "conv_b": hyb_conv_b[0].reshape(1, d_rnn),
        "gate_w": _gate_pairs(0.5 * hyb_rg_w[0], 0.5 * hyb_ig_w[0]),
        "rg_b": 0.5 * hyb_rg_b[0].reshape(1, d_rnn),
        "ig_b": 0.5 * hyb_ig_b[0].reshape(1, d_rnn),
        "lam": hyb_lambda[0].reshape(1, d_rnn),
        "w_out": hyb_w_out[0].astype(BF16),
        "pool_w": pool_w[0].astype(BF16),
        "pool_scale": pool_scale[0].reshape(1, d),
        "norm_mix": norm_mix,
        "norm_ffn": norm_ffn,
        "ffn_gate": ffn_gate.astype(BF16),
        "ffn_up": ffn_up.astype(BF16),
        "ffn_down": ffn_down.astype(BF16),
        "norm_final": norm_final.reshape(1, d),
    }
    b = x_prompt.shape[0]
    dt = x_prompt.dtype
    y_p, k_p, v_p, h_p, conv_p, pool_p = _trunk(
        x_prompt, None, None, jnp.zeros((b, d_rnn), dt), jnp.zeros((b, CONV_W - 1, d_rnn), dt),
        jnp.zeros((b, POOL_HIST - 1, d), dt), 0, w)
    db, past = cache_sb_k.shape[1], cache_sb_k.shape[2]
    past_k, past_v = (c[0].transpose(0, 2, 3, 1).reshape(db, SB_WIDTH, past) for c in (cache_sb_k, cache_sb_v))
    y_s, k_s, v_s, h_s, conv_s, pool_s = _trunk(
        x_sample, past_k, past_v, state_lru_h[0], state_lru_conv[0], state_pool[0], past, w)
    return (y_p, y_s, k_p, v_p, h_p, conv_p, pool_p, k_s, v_s, h_s, conv_s, pool_s)
```

```python
import functools

import jax
import jax.numpy as jnp
from jax import lax
from jax.experimental import pallas as pl
from jax.experimental.pallas import tpu as pltpu

F32 = jnp.float32
BF16 = jnp.bfloat16

N_SB_HEADS = 8
SB_HEAD_DIM = 64
SB_WIDTH = N_SB_HEADS * SB_HEAD_DIM
SB_SCALE = SB_HEAD_DIM ** -0.5
N_LRU_BLOCKS = 8
CONV_W = 4
LRU_C = 8.0
POOL_WINDOWS = (2, 4, 8, 16)
POOL_HIST = 16
EPS = 1e-6
LOG2_E = 1.4426950408889634

LANES = 128
SUBLANES = 8
HEAD_PAIRS = SB_WIDTH // LANES
KEY_BLOCK = LANES
EXP_ZERO_BELOW = -104.0
VMEM_LIMIT_BYTES = 56 * 1024 * 1024


def _cparams(n_axes):
    return pltpu.CompilerParams(dimension_semantics=("arbitrary",) * n_axes,
                                vmem_limit_bytes=VMEM_LIMIT_BYTES)


def _resident(shape):
    zeros = (0,) * len(shape)
    return pl.BlockSpec(shape, lambda *_: zeros, pipeline_mode=pl.Buffered(1))


def _rms(x, gain):
    ms = jnp.mean(x * x, axis=-1, keepdims=True)
    return (x * lax.rsqrt(ms + EPS)) * gain


def _dot(a, b):
    return jnp.dot(a, b, preferred_element_type=F32)


def _dot_nt(a, b):
    return lax.dot_general(a, b, (((1,), (1,)), ((), ())), preferred_element_type=F32)


def _in_proj_kernel(x_ref, gain_ref, wq_ref, wkv_ref, wug_ref, q_ref, k_ref, v_ref, kb_ref, vb_ref,
                    u_ref, g_ref, *, keys_on_lanes):
    xb = _rms(x_ref[...], gain_ref[...]).astype(BF16)
    w = SB_WIDTH
    q_ref[...] = (_dot(xb, wq_ref[...]) * SB_SCALE).astype(BF16)
    if keys_on_lanes:
        for dst, dst_b, rows in ((k_ref, kb_ref, slice(0, w)), (v_ref, vb_ref, slice(w, 2 * w))):
            t = _dot_nt(wkv_ref[rows, :], xb)
            dst[...] = t
            for blk in range(dst_b.shape[0]):
                dst_b[blk] = t[:, blk * KEY_BLOCK:(blk + 1) * KEY_BLOCK].astype(BF16)
    else:
        for dst, dst_b, cols in ((k_ref, kb_ref, slice(0, w)), (v_ref, vb_ref, slice(w, 2 * w))):
            t = _dot(xb, wkv_ref[:, cols])
            dst[...] = t
            dst_b[...] = t.astype(BF16)
    d_rnn = u_ref.shape[-1]
    u_ref[...] = _dot(xb, wug_ref[:, 0:d_rnn])
    g_ref[...] = _dot(xb, wug_ref[:, d_rnn:2 * d_rnn])


def _in_proj(x2d, gain, wq, wkv, wug, tm, batch, seq, keys_on_lanes):
    rows, d = x2d.shape
    d_rnn = wug.shape[1] // 2
    row = lambda width: pl.BlockSpec((tm, width), lambda i: (i, 0))
    if keys_on_lanes:
        assert seq % tm == 0 and tm % KEY_BLOCK == 0
        tpb, kpt = seq // tm, tm // KEY_BLOCK
        kv = pl.BlockSpec((None, SB_WIDTH, tm), lambda i: (i // tpb, 0, i % tpb))
        kvb = pl.BlockSpec((None, kpt, SB_WIDTH, KEY_BLOCK), lambda i: (i // tpb, i % tpb, 0, 0))
        kv_shape = jax.ShapeDtypeStruct((batch, SB_WIDTH, seq), F32)
        kvb_shape = jax.ShapeDtypeStruct((batch, seq // KEY_BLOCK, SB_WIDTH, KEY_BLOCK), BF16)
    else:
        kv = kvb = row(SB_WIDTH)
        kv_shape = jax.ShapeDtypeStruct((rows, SB_WIDTH), F32)
        kvb_shape = jax.ShapeDtypeStruct((rows, SB_WIDTH), BF16)
    return pl.pallas_call(
        functools.partial(_in_proj_kernel, keys_on_lanes=keys_on_lanes),
        grid=(rows // tm,),
        in_specs=[row(d), _resident((1, d)), _resident(wq.shape), _resident(wkv.shape), _resident(wug.shape)],
        out_specs=[row(SB_WIDTH), kv, kv, kvb, kvb, row(d_rnn), row(d_rnn)],
        out_shape=[jax.ShapeDtypeStruct((rows, SB_WIDTH), BF16), kv_shape, kv_shape, kvb_shape, kvb_shape,
                   jax.ShapeDtypeStruct((rows, d_rnn), F32), jax.ShapeDtypeStruct((rows, d_rnn), F32)],
        compiler_params=_cparams(1),
        name="in_proj",
    )(x2d, gain, wq, wkv, wug)


def _cumsum_matrix():
    s = lax.broadcasted_iota(jnp.int32, (KEY_BLOCK, 2 * KEY_BLOCK), 0)
    j = lax.broadcasted_iota(jnp.int32, (KEY_BLOCK, 2 * KEY_BLOCK), 1)
    return ((s > j) | (j >= KEY_BLOCK)).astype(BF16)


def _sb_begin(q_ref, qq_ref, acc_ref, car_ref):
    ns, tq, _ = q_ref.shape
    acc_ref[...] = jnp.zeros_like(acc_ref)
    car_ref[...] = jnp.zeros_like(car_ref)
    low_half = lax.broadcasted_iota(jnp.int32, (tq, LANES), 1) < SB_HEAD_DIM
    for s in range(ns):
        for p in range(HEAD_PAIRS):
            q2 = q_ref[s, :, p * LANES:(p + 1) * LANES]
            zero = jnp.zeros_like(q2)
            qq_ref[s, p, 0:tq, :] = jnp.where(low_half, q2, zero)
            qq_ref[s, p, tq:2 * tq, :] = jnp.where(low_half, zero, q2)


def _sb_tile(qq_ref, kblks, vblks, acc_ref, car_ref, *, diag, keys_on_lanes):
    ns, _, tq2, _ = qq_ref.shape
    tq = tq2 // 2
    assert tq & (tq - 1) == 0
    cm = _cumsum_matrix()
    cm2 = jnp.concatenate([cm, cm], axis=0)
    pair = lambda p: slice(p * LANES, (p + 1) * LANES)
    zs = []
    for s in range(ns):
        for p in range(HEAD_PAIRS):
            if keys_on_lanes:
                zs.append(_dot(qq_ref[s, p], kblks[s][pair(p), :]))
            else:
                zs.append(_dot_nt(qq_ref[s, p], kblks[s][:, pair(p)]))
    z = jnp.concatenate(zs, axis=0)
    soft = jnp.log(1.0 + jnp.exp2(jnp.abs(z) * -LOG2_E))
    log_beta = jnp.minimum(z, 0.0) - soft
    log_not = log_beta - z
    if diag:
        key = lax.broadcasted_iota(jnp.int32, z.shape, 1)
        query = lax.broadcasted_iota(jnp.int32, z.shape, 0) & (tq - 1)
        before = key < query
        log_not = jnp.where(before, log_not, 0.0)
    hi = log_not.astype(BF16)
    lo = (log_not - hi.astype(F32)).astype(BF16)
    sums = _dot(jnp.concatenate([hi, lo], axis=1), cm2)
    car = car_ref[...]
    wgt = jnp.exp(log_beta + sums[:, :KEY_BLOCK] + car)
    if diag:
        wgt = jnp.where(before, wgt, 0.0)
    car_ref[...] = car + sums[:, KEY_BLOCK:]
    wb = wgt.astype(BF16)
    low_half = lax.broadcasted_iota(jnp.int32, (tq, LANES), 1) < SB_HEAD_DIM
    for s in range(ns):
        for p in range(HEAD_PAIRS):
            r0 = (s * HEAD_PAIRS + p) * 2 * tq
            w2 = wb[r0:r0 + 2 * tq, :]
            pv = _dot_nt(w2, vblks[s][pair(p), :]) if keys_on_lanes else _dot(w2, vblks[s][:, pair(p)])
            acc_ref[s, p] += jnp.where(low_half, pv[0:tq, :], pv[tq:2 * tq, :])


def _sb_all_faded(car_ref):
    return jnp.max(car_ref[...]) < EXP_ZERO_BELOW


def _sb_write(o_ref, acc_ref):
    for s in range(acc_ref.shape[0]):
        for p in range(HEAD_PAIRS):
            o_ref[s, :, p * LANES:(p + 1) * LANES] = acc_ref[s, p].astype(o_ref.dtype)


def _sb_scratch(ns, tq):
    return [pltpu.VMEM((ns, HEAD_PAIRS, 2 * tq, LANES), BF16),
            pltpu.VMEM((ns, HEAD_PAIRS, tq, LANES), F32),
            pltpu.VMEM((ns * N_SB_HEADS * tq, LANES), F32)]


def _sb_kernel(q_ref, k0_ref, v0_ref, pk_hbm, pv_hbm, o_ref, qq_ref, acc_ref, car_ref,
               kbuf, vbuf, sem, *, fresh):
    ns = q_ref.shape[0]
    b0 = pl.program_id(0) * ns
    if fresh:
        newest = pl.program_id(1) - 1
        past_block = lambda hbm, s, blk: hbm.at[b0 + s, blk]
    else:
        newest = pk_hbm.shape[2] // KEY_BLOCK - 1
        past_block = lambda hbm, s, blk: hbm.at[b0 + s, :, pl.ds(pl.multiple_of(blk * KEY_BLOCK, KEY_BLOCK),
                                                                 KEY_BLOCK)]

    def copies(blk):
        slot = blk & 1
        return [pltpu.make_async_copy(past_block(hbm, s, blk), buf.at[slot, s], sem.at[a, slot, s])
                for a, (hbm, buf) in enumerate(((pk_hbm, kbuf), (pv_hbm, vbuf))) for s in range(ns)]

    def start(blk):
        for c in copies(blk):
            c.start()

    def wait(blk):
        for c in copies(blk):
            c.wait()

    @pl.when(newest >= 0)
    def _():
        start(newest)

    _sb_begin(q_ref, qq_ref, acc_ref, car_ref)
    if fresh:
        first_k, first_v = [k0_ref[s, 0] for s in range(ns)], [v0_ref[s, 0] for s in range(ns)]
    else:
        first_k, first_v = [k0_ref[s] for s in range(ns)], [v0_ref[s] for s in range(ns)]
    _sb_tile(qq_ref, first_k, first_v, acc_ref, car_ref, diag=True, keys_on_lanes=fresh)

    def more(state):
        blk, faded = state
        return jnp.logical_and(blk >= 0, jnp.logical_not(faded))

    def step(state):
        blk, _ = state
        wait(blk)

        @pl.when(blk >= 1)
        def _():
            start(blk - 1)

        slot = blk & 1
        _sb_tile(qq_ref, [kbuf[slot, s].astype(BF16) for s in range(ns)],
                 [vbuf[slot, s].astype(BF16) for s in range(ns)], acc_ref, car_ref,
                 diag=False, keys_on_lanes=True)
        return blk - 1, _sb_all_faded(car_ref)

    blk, _ = lax.while_loop(more, step, (newest, _sb_all_faded(car_ref)))

    @pl.when(blk >= 0)
    def _():
        wait(blk)

    _sb_write(o_ref, acc_ref)


def _sb_attention(q, k0, v0, past_k, past_v, ns, fresh):
    batch, seq, _ = q.shape
    assert batch % ns == 0
    hbm = pl.BlockSpec(memory_space=pl.ANY)
    if fresh:
        assert seq % KEY_BLOCK == 0
        tq, grid = KEY_BLOCK, (batch // ns, seq // KEY_BLOCK)
        qspec = pl.BlockSpec((ns, tq, SB_WIDTH), lambda b, i: (b, i, 0))
        nspec = pl.BlockSpec((ns, 1, SB_WIDTH, KEY_BLOCK), lambda b, i: (b, i, 0, 0))
    else:
        past = past_k.shape[2]
        assert seq <= KEY_BLOCK and seq % 16 == 0 and past % KEY_BLOCK == 0 and past > 0
        pad = ((0, 0), (0, KEY_BLOCK - seq), (0, 0))
        k0, v0 = jnp.pad(k0, pad), jnp.pad(v0, pad)
        tq, grid = seq, (batch // ns,)
        qspec = pl.BlockSpec((ns, tq, SB_WIDTH), lambda b: (b, 0, 0))
        nspec = pl.BlockSpec((ns, KEY_BLOCK, SB_WIDTH), lambda b: (b, 0, 0))
    stage = pltpu.VMEM((2, ns, SB_WIDTH, KEY_BLOCK), past_k.dtype)
    return pl.pallas_call(
        functools.partial(_sb_kernel, fresh=fresh),
        grid=grid,
        in_specs=[qspec, nspec, nspec, hbm, hbm],
        out_specs=qspec,
        out_shape=jax.ShapeDtypeStruct(q.shape, BF16),
        scratch_shapes=_sb_scratch(ns, tq) + [stage, stage, pltpu.SemaphoreType.DMA((2, 2, ns))],
        compiler_params=_cparams(len(grid)),
        name="sb_prompt" if fresh else "sb_sample",
    )(q, k0, v0, past_k, past_v)


def _lru_kernel(u_ref, g_ref, h0_ref, c0_ref, cw_ref, cb_ref, gw_ref, rb_ref, ib_ref, lam_ref,
                o_ref, hl_ref, cn_ref, h_ref, ext_ref, a_ref, x_ref, hseq_ref):
    i = pl.program_id(1)
    nb, tc, d = u_ref.shape
    n = nb * tc
    hist = SUBLANES

    @pl.when(i == 0)
    def _():
        h_ref[...] = h0_ref[...]
        ext_ref[:, 0:hist, :] = c0_ref[...]

    u3 = u_ref[...]
    ext_ref[:, hist:, :] = u3
    uc = cb_ref[...] + u3.reshape(n, d) * cw_ref[CONV_W - 1:CONV_W, :]
    for back in range(1, CONV_W):
        prev = ext_ref[:, hist - back:hist - back + tc, :].reshape(n, d)
        uc = uc + prev * cw_ref[CONV_W - 1 - back:CONV_W - back, :]
    tail = ext_ref[:, tc:tc + hist, :]
    ext_ref[:, 0:hist, :] = tail

    ucb = uc.astype(BF16)
    gates = [_dot(ucb[:, p * LANES:(p + 1) * LANES], gw_ref[p]) for p in range(d // LANES)]
    tr = jnp.tanh(jnp.concatenate([gt[:, :LANES] for gt in gates], axis=1) + rb_ref[...])
    ti = jnp.tanh(jnp.concatenate([gt[:, LANES:] for gt in gates], axis=1) + ib_ref[...])
    lam = lam_ref[...]
    softplus_neg = jnp.maximum(-lam, 0.0) + jnp.log1p(jnp.exp(-jnp.abs(lam)))
    log_a = (tr + 1.0) * ((-0.5 * LRU_C) * softplus_neg)
    a = jnp.exp(log_a)
    th = jnp.tanh(log_a)
    p = -2.0 * th
    mult = p * lax.rsqrt(jnp.maximum(p * (1.0 - th), jnp.finfo(F32).tiny))
    x_in = (mult * uc) * (0.5 * ti + 0.5)
    nlb = d // LANES
    pitch = a_ref.shape[1] // nb
    for c in range(nlb):
        for b in range(nb):
            dst = slice(b * pitch, b * pitch + tc)
            a_ref[c, dst, :] = a[b * tc:(b + 1) * tc, c * LANES:(c + 1) * LANES]
            x_ref[c, dst, :] = x_in[b * tc:(b + 1) * tc, c * LANES:(c + 1) * LANES]

    hs = [h_ref[:, c * LANES:(c + 1) * LANES] for c in range(nlb)]
    for t in range(tc):
        step = pl.ds(t, nb, stride=pitch)
        for c in range(nlb):
            hs[c] = a_ref[c, step, :] * hs[c] + x_ref[c, step, :]
            hseq_ref[c, step, :] = hs[c]
    h = jnp.concatenate(hs, axis=1)
    h_ref[...] = h
    h_all = jnp.concatenate(
        [jnp.concatenate([hseq_ref[c, b * pitch:b * pitch + tc, :] for b in range(nb)], axis=0)
         for c in range(nlb)], axis=1)

    g = g_ref[...].reshape(n, d)
    inner = g * (0.7978845608028654 + (0.7978845608028654 * 0.044715) * (g * g))
    o_ref[...] = ((h_all * g) * (0.5 * jnp.tanh(inner) + 0.5)).astype(o_ref.dtype).reshape(nb, tc, d)

    @pl.when(i == pl.num_programs(1) - 1)
    def _():
        hl_ref[...] = h
        cn_ref[...] = tail


def _lru(u, g, h0, conv0, conv_w, conv_b, gate_w, rg_b, ig_b, lam, tc):
    batch, seq, d_rnn = u.shape
    nb = SUBLANES
    assert batch % nb == 0 and seq % tc == 0 and tc % 16 == 0 and seq >= CONV_W - 1
    conv0p = jnp.pad(conv0, ((0, 0), (SUBLANES - (CONV_W - 1), 0), (0, 0)))
    chunk = pl.BlockSpec((nb, tc, d_rnn), lambda b, i: (b, i, 0))
    state = pl.BlockSpec((nb, d_rnn), lambda b, i: (b, 0))
    tails = pl.BlockSpec((nb, SUBLANES, d_rnn), lambda b, i: (b, 0, 0))
    vec = _resident((1, d_rnn))
    pitch = tc + SUBLANES if (tc // SUBLANES) % 2 == 0 else tc
    scan_buf = pltpu.VMEM((d_rnn // LANES, nb * pitch, LANES), F32)
    out, h_last, conv_new = pl.pallas_call(
        _lru_kernel,
        grid=(batch // nb, seq // tc),
        in_specs=[chunk, chunk, state, tails, _resident(conv_w.shape), vec,
                  _resident(gate_w.shape), vec, vec, vec],
        out_specs=[chunk, state, tails],
        out_shape=[jax.ShapeDtypeStruct(u.shape, BF16),
                   jax.ShapeDtypeStruct((batch, d_rnn), F32),
                   jax.ShapeDtypeStruct((batch, SUBLANES, d_rnn), F32)],
        scratch_shapes=[pltpu.VMEM((nb, d_rnn), F32),
                        pltpu.VMEM((nb, SUBLANES + tc, d_rnn), F32),
                        scan_buf, scan_buf, scan_buf],
        compiler_params=_cparams(2),
        name="rg_lru",
    )(u, g, h0, conv0p, conv_w, conv_b, gate_w, rg_b, ig_b, lam)
    return out, h_last, conv_new[:, SUBLANES - (CONV_W - 1):, :]


def _ffn(xb, wg_ref, wu_ref, wd_ref, chunk):
    d_ff = wg_ref.shape[1]
    acc = None
    for c in range(0, d_ff, chunk):
        gate = _dot(xb, wg_ref[:, c:c + chunk])
        up = _dot(xb, wu_ref[:, c:c + chunk])
        hid = (gate * jax.nn.sigmoid(gate) * up).astype(BF16)
        part = _dot(hid, wd_ref[c:c + chunk, :])
        acc = part if acc is None else acc + part
    return acc


def _ffn_chunk(d_ff):
    for chunk in (512, 256, 128):
        if d_ff % chunk == 0:
            return chunk
    return d_ff


def _mix_ffn_kernel(a_ref, l_ref, x_ref, wo_ref, gain_ref, wg_ref, wu_ref, wd_ref, o_ref, *, chunk):
    sbw = a_ref.shape[1]
    y = _dot(a_ref[...], wo_ref[0:sbw, :]) + _dot(l_ref[...], wo_ref[sbw:, :])
    x1 = x_ref[...] + y
    xb = _rms(x1, gain_ref[...]).astype(BF16)
    o_ref[...] = x1 + _ffn(xb, wg_ref, wu_ref, wd_ref, chunk)


def _mix_ffn(attn, lru_out, x2d, w_out, gain, wg, wu, wd, tm):
    rows, d = x2d.shape
    row = lambda width: pl.BlockSpec((tm, width), lambda i: (i, 0))
    return pl.pallas_call(
        functools.partial(_mix_ffn_kernel, chunk=_ffn_chunk(wg.shape[1])),
        grid=(rows // tm,),
        in_specs=[row(attn.shape[1]), row(lru_out.shape[1]), row(d), _resident(w_out.shape),
                  _resident((1, d)), _resident(wg.shape), _resident(wu.shape), _resident(wd.shape)],
        out_specs=row(d),
        out_shape=jax.ShapeDtypeStruct((rows, d), F32),
        compiler_params=_cparams(1),
        name="mix_ffn",
    )(attn, lru_out, x2d, w_out, gain, wg, wu, wd)


def _pool_delta(ext, n, pos0):
    d = ext.shape[1]
    gw = d // len(POOL_WINDOWS)
    pos = pos0 + lax.broadcasted_iota(jnp.int32, (n, 1), 0)
    cur = ext[POOL_HIST:, :]
    sums, s, col0 = [], ext, 0
    for w in POOL_WINDOWS:
        half = w // 2
        s = s[half:, col0:] + s[:-half, col0:]
        first = POOL_HIST - (w - 1)
        sums.append(s[first:first + n, :gw])
        col0 = gw
    outs = []
    for g, w in enumerate(POOL_WINDOWS):
        inv = 1.0 / jnp.minimum(w, pos + 1).astype(F32)
        outs.append(sums[g] * inv - cur[:, g * gw:(g + 1) * gw])
    return outs


def _pool_ffn_kernel(x_ref, buf_ref, gm_ref, pw_ref, ps_ref, gf_ref, wg_ref, wu_ref, wd_ref,
                     gl_ref, o_ref, pn_ref, hist_ref, d_ref, *, chunk, segs_per_batch, start_pos):
    nb, seg, d = x_ref.shape
    gw = d // len(POOL_WINDOWS)
    i = pl.program_id(0)
    x = x_ref[...].reshape(nb * seg, d)
    xn = _rms(x, gm_ref[...])
    if segs_per_batch == 1:
        seg_idx = 0
    else:
        seg_idx = i % segs_per_batch

        @pl.when(seg_idx == 0)
        def _():
            hist_ref[...] = buf_ref[0]

    for b in range(nb):
        hist = buf_ref[b] if segs_per_batch == 1 else hist_ref[...]
        ext = jnp.concatenate([hist, xn[b * seg:(b + 1) * seg, :]], axis=0)
        for g, dg in enumerate(_pool_delta(ext, seg, start_pos + seg_idx * seg)):
            d_ref[b * seg:(b + 1) * seg, g * gw:(g + 1) * gw] = dg.astype(BF16)
        tail = ext[seg:, :]
        if segs_per_batch == 1:
            pn_ref[b] = tail
        else:
            hist_ref[...] = tail

            @pl.when(seg_idx == segs_per_batch - 1)
            def _():
                pn_ref[0] = tail

    y = jnp.concatenate([_dot(d_ref[:, g * gw:(g + 1) * gw], pw_ref[g])
                         for g in range(len(POOL_WINDOWS))], axis=1)
    x1 = x + y * ps_ref[...]
    xb = _rms(x1, gf_ref[...]).astype(BF16)
    x2 = x1 + _ffn(xb, wg_ref, wu_ref, wd_ref, chunk)
    o_ref[...] = _rms(x2, gl_ref[...]).reshape(nb, seg, d)


def _pool_ffn(x2d, pool0, start_pos, gain_mix, pool_w, pool_scale, gain_ffn, wg, wu, wd,
              gain_final, batch, seq, seg, nb):
    d = x2d.shape[1]
    spb = seq // seg
    assert seq % seg == 0 and seg % SUBLANES == 0 and (nb == 1 or spb == 1) and batch % nb == 0
    nseg = batch * spb
    buf = jnp.pad(pool0, ((0, 0), (1, 0), (0, 0)))
    xspec = pl.BlockSpec((nb, seg, d), lambda i: (i, 0, 0))
    bspec = pl.BlockSpec((nb, POOL_HIST, d), (lambda i: (i, 0, 0)) if spb == 1 else (lambda i: (i // spb, 0, 0)))
    vec = _resident((1, d))
    y, pool_new = pl.pallas_call(
        functools.partial(_pool_ffn_kernel, chunk=_ffn_chunk(wg.shape[1]), segs_per_batch=spb,
                          start_pos=start_pos),
        grid=(nseg // nb,),
        in_specs=[xspec, bspec, vec, _resident(pool_w.shape), vec, vec, _resident(wg.shape),
                  _resident(wu.shape), _resident(wd.shape), vec],
        out_specs=[xspec, bspec],
        out_shape=[jax.ShapeDtypeStruct((nseg, seg, d), F32),
                   jax.ShapeDtypeStruct((batch, POOL_HIST, d), F32)],
        scratch_shapes=[pltpu.VMEM((POOL_HIST, d), F32), pltpu.VMEM((nb * seg, d), BF16)],
        compiler_params=_cparams(1),
        name="pool_ffn",
    )(x2d.reshape(nseg, seg, d), buf, gain_mix, pool_w, pool_scale, gain_ffn, wg, wu, wd, gain_final)
    return y.reshape(batch * seq, d), pool_new[:, 1:, :]


def _pick(n, candidates):
    for c in candidates:
        if n % c == 0:
            return c
    return n


def _gate_pairs(rg_w, ig_w):
    nblk, c, _ = rg_w.shape
    z = jnp.zeros((c, c), rg_w.dtype)

    def pair(w, p):
        return jnp.block([[w[2 * p], z], [z, w[2 * p + 1]]])

    return jnp.stack([jnp.concatenate([pair(rg_w, p), pair(ig_w, p)], axis=1)
                      for p in range(nblk // 2)]).astype(BF16)


def _trunk(x, past_k, past_v, h0, conv0, pool0, start_pos, w):
    batch, seq, d = x.shape
    rows = batch * seq
    x2d = x.reshape(rows, d)
    tm = _pick(rows, (512, 256, 128, 64, 32, 16, 8))
    fresh = past_k is None

    q, k, v, kb, vb, u, g = _in_proj(x2d, w["norm_mix"][0:1], w["wq"], w["wkv_t"] if fresh else w["wkv"],
                                     w["wug"], tm, batch, seq, keys_on_lanes=fresh)
    q3 = q.reshape(batch, seq, SB_WIDTH)
    ns = _pick(batch, (8, 4, 2, 1))
    if fresh:
        attn = _sb_attention(q3, kb, vb, kb, vb, ns, fresh=True)
    else:
        attn = _sb_attention(q3, kb.reshape(q3.shape), vb.reshape(q3.shape), past_k, past_v, ns, fresh=False)
    attn = attn.reshape(rows, SB_WIDTH)
    d_rnn = u.shape[1]
    lru_out, h_last, conv_new = _lru(u.reshape(batch, seq, d_rnn), g.reshape(batch, seq, d_rnn), h0, conv0,
                                     w["conv_w"], w["conv_b"], w["gate_w"], w["rg_b"], w["ig_b"], w["lam"],
                                     _pick(seq, (128, 64, 32, 16)))
    lru_out = lru_out.reshape(rows, d_rnn)
    x1 = _mix_ffn(attn, lru_out, x2d, w["w_out"], w["norm_ffn"][0:1], w["ffn_gate"][0],
                  w["ffn_up"][0], w["ffn_down"][0], tm)
    if seq >= 512:
        seg, nb = _pick(seq, (512, 256, 128)), 1
    else:
        seg, nb = seq, _pick(batch, (8, 4, 2, 1))
    y, pool_new = _pool_ffn(x1, pool0, start_pos, w["norm_mix"][1:2], w["pool_w"], w["pool_scale"],
                            w["norm_ffn"][1:2], w["ffn_gate"][1], w["ffn_up"][1], w["ffn_down"][1],
                            w["norm_final"], batch, seq, seg, nb)
    heads = (N_SB_HEADS, SB_HEAD_DIM)
    if fresh:
        k, v = (t.reshape(1, batch, *heads, seq).transpose(0, 1, 4, 2, 3) for t in (k, v))
    else:
        k, v = (t.reshape(1, batch, seq, *heads) for t in (k, v))
    return (y.reshape(batch, seq, d), k, v, h_last[None], conv_new[None], pool_new[None])


def kernel(x_prompt, x_sample, cache_sb_k, cache_sb_v, state_lru_h, state_lru_conv, state_pool,
           hyb_w_in, hyb_conv_w, hyb_conv_b, hyb_rg_w, hyb_rg_b, hyb_ig_w, hyb_ig_b, hyb_lambda,
           hyb_w_out, pool_w, pool_scale, norm_mix, norm_ffn, ffn_gate, ffn_up, ffn_down, norm_final):
    assert norm_mix.shape[0] == 2 and hyb_w_in.shape[0] == 1 and pool_w.shape[0] == 1
    d = x_prompt.shape[-1]
    d_rnn = hyb_lambda.shape[-1]
    w_in = hyb_w_in[0].astype(BF16)
    w = {
        "wq": w_in[:, 0:SB_WIDTH],
        "wkv": w_in[:, SB_WIDTH:3 * SB_WIDTH],
        "wkv_t": w_in[:, SB_WIDTH:3 * SB_WIDTH].T,
        "wug": w_in[:, 3 * SB_WIDTH:],
        "conv_w": hyb_conv_w[0],
        "conv_b": hyb_conv_b[0].reshape(1, d_rnn),
        "gate_w": _gate_pairs(0.5 * hyb_rg_w[0], 0.5 * hyb_ig_w[0]),
        "rg_b": 0.5 * hyb_rg_b[0].reshape(1, d_rnn),
        "ig_b": 0.5 * hyb_ig_b[0].reshape(1, d_rnn),
        "lam": hyb_lambda[0].reshape(1, d_rnn),
        "w_out": hyb_w_out[0].astype(BF16),
        "pool_w": pool_w[0].astype(BF16),
        "pool_scale": pool_scale[0].reshape(1, d),
        "norm_mix": norm_mix,
        "norm_ffn": norm_ffn,
        "ffn_gate": ffn_gate.astype(BF16),
        "ffn_up": ffn_up.astype(BF16),
        "ffn_down": ffn_down.astype(BF16),
        "norm_final": norm_final.reshape(1, d),
    }
    b = x_prompt.shape[0]
    dt = x_prompt.dtype
    y_p, k_p, v_p, h_p, conv_p, pool_p = _trunk(
        x_prompt, None, None, jnp.zeros((b, d_rnn), dt), jnp.zeros((b, CONV_W - 1, d_rnn), dt),
        jnp.zeros((b, POOL_HIST - 1, d), dt), 0, w)
    db, past = cache_sb_k.shape[1], cache_sb_k.shape[2]
    past_k, past_v = (c[0].transpose(0, 2, 3, 1).reshape(db, SB_WIDTH, past) for c in (cache_sb_k, cache_sb_v))
    y_s, k_s, v_s, h_s, conv_s, pool_s = _trunk(
        x_sample, past_k, past_v, state_lru_h[0], state_lru_conv[0], state_pool[0], past, w)
    return (y_p, y_s, k_p, v_p, h_p, conv_p, pool_p, k_s, v_s, h_s, conv_s, pool_s)
```

```python
import functools

import jax
import jax.numpy as jnp
from jax import lax
from jax.experimental import pallas as pl
from jax.experimental.pallas import tpu as pltpu

F32 = jnp.float32
BF16 = jnp.bfloat16

N_SB_HEADS = 8
SB_HEAD_DIM = 64
SB_WIDTH = N_SB_HEADS * SB_HEAD_DIM
SB_SCALE = SB_HEAD_DIM ** -0.5
N_LRU_BLOCKS = 8
CONV_W = 4
LRU_C = 8.0
POOL_WINDOWS = (2, 4, 8, 16)
POOL_HIST = 16
EPS = 1e-6
LOG2_E = 1.4426950408889634

LANES = 128
SUBLANES = 8
HEAD_PAIRS = SB_WIDTH // LANES
KEY_BLOCK = LANES
EXP_ZERO_BELOW = -104.0
VMEM_LIMIT_BYTES = 56 * 1024 * 1024


def _cparams(n_axes):
    return pltpu.CompilerParams(dimension_semantics=("arbitrary",) * n_axes,
                                vmem_limit_bytes=VMEM_LIMIT_BYTES)


def _resident(shape):
    zeros = (0,) * len(shape)
    return pl.BlockSpec(shape, lambda *_: zeros, pipeline_mode=pl.Buffered(1))


def _layer_of(stack, layer):
    return pl.BlockSpec((None,) + stack.shape[1:], lambda *_: (layer, 0, 0), pipeline_mode=pl.Buffered(1))


def _rms(x, gain):
    ms = jnp.mean(x * x, axis=-1, keepdims=True)
    return (x * lax.rsqrt(ms + EPS)) * gain


def _dot(a, b):
    return jnp.dot(a, b, preferred_element_type=F32)


def _dot_nt(a, b):
    return lax.dot_general(a, b, (((1,), (1,)), ((), ())), preferred_element_type=F32)


def _in_proj_kernel(x_ref, gain_ref, wq_ref, wkv_ref, wug_ref, q_ref, k_ref, v_ref, kb_ref, vb_ref,
                    u_ref, g_ref, *, keys_on_lanes):
    xb = _rms(x_ref[...], gain_ref[...]).astype(BF16)
    w = SB_WIDTH
    q_ref[...] = (_dot(xb, wq_ref[...]) * SB_SCALE).astype(BF16)
    if keys_on_lanes:
        for dst, dst_b, rows in ((k_ref, kb_ref, slice(0, w)), (v_ref, vb_ref, slice(w, 2 * w))):
            t = _dot_nt(wkv_ref[rows, :], xb)
            dst[...] = t
            for blk in range(dst_b.shape[0]):
                dst_b[blk] = t[:, blk * KEY_BLOCK:(blk + 1) * KEY_BLOCK].astype(BF16)
    else:
        for dst, dst_b, cols in ((k_ref, kb_ref, slice(0, w)), (v_ref, vb_ref, slice(w, 2 * w))):
            t = _dot(xb, wkv_ref[:, cols])
            dst[...] = t
            dst_b[...] = t.astype(BF16)
    d_rnn = u_ref.shape[-1]
    u_ref[...] = _dot(xb, wug_ref[:, 0:d_rnn])
    g_ref[...] = _dot(xb, wug_ref[:, d_rnn:2 * d_rnn])


def _in_proj(x2d, gain, wq, wkv, wug, tm, batch, seq, keys_on_lanes):
    rows, d = x2d.shape
    d_rnn = wug.shape[1] // 2
    row = lambda width: pl.BlockSpec((tm, width), lambda i: (i, 0))
    if keys_on_lanes:
        assert seq % tm == 0 and tm % KEY_BLOCK == 0
        tpb, kpt = seq // tm, tm // KEY_BLOCK
        kv = pl.BlockSpec((None, SB_WIDTH, tm), lambda i: (i // tpb, 0, i % tpb))
        kvb = pl.BlockSpec((None, kpt, SB_WIDTH, KEY_BLOCK), lambda i: (i // tpb, i % tpb, 0, 0))
        kv_shape = jax.ShapeDtypeStruct((batch, SB_WIDTH, seq), F32)
        kvb_shape = jax.ShapeDtypeStruct((batch, seq // KEY_BLOCK, SB_WIDTH, KEY_BLOCK), BF16)
    else:
        kv = kvb = row(SB_WIDTH)
        kv_shape = jax.ShapeDtypeStruct((rows, SB_WIDTH), F32)
        kvb_shape = jax.ShapeDtypeStruct((rows, SB_WIDTH), BF16)
    return pl.pallas_call(
        functools.partial(_in_proj_kernel, keys_on_lanes=keys_on_lanes),
        grid=(rows // tm,),
        in_specs=[row(d), _resident((1, d)), _resident(wq.shape), _resident(wkv.shape), _resident(wug.shape)],
        out_specs=[row(SB_WIDTH), kv, kv, kvb, kvb, row(d_rnn), row(d_rnn)],
        out_shape=[jax.ShapeDtypeStruct((rows, SB_WIDTH), BF16), kv_shape, kv_shape, kvb_shape, kvb_shape,
                   jax.ShapeDtypeStruct((rows, d_rnn), F32), jax.ShapeDtypeStruct((rows, d_rnn), F32)],
        compiler_params=_cparams(1),
        name="in_proj",
    )(x2d, gain, wq, wkv, wug)


def _cumsum_matrix():
    s = lax.broadcasted_iota(jnp.int32, (KEY_BLOCK, 2 * KEY_BLOCK), 0)
    j = lax.broadcasted_iota(jnp.int32, (KEY_BLOCK, 2 * KEY_BLOCK), 1)
    return ((s > j) | (j >= KEY_BLOCK)).astype(BF16)


def _sb_begin(q_ref, qq_ref, acc_ref, car_ref):
    ns, tq, _ = q_ref.shape
    acc_ref[...] = jnp.zeros_like(acc_ref)
    car_ref[...] = jnp.zeros_like(car_ref)
    low_half = lax.broadcasted_iota(jnp.int32, (tq, LANES), 1) < SB_HEAD_DIM
    for s in range(ns):
        for p in range(HEAD_PAIRS):
            q2 = q_ref[s, :, p * LANES:(p + 1) * LANES]
            zero = jnp.zeros_like(q2)
            qq_ref[s, p, 0:tq, :] = jnp.where(low_half, q2, zero)
            qq_ref[s, p, tq:2 * tq, :] = jnp.where(low_half, zero, q2)


def _sb_tile(qq_ref, kblks, vblks, acc_ref, car_ref, *, diag, keys_on_lanes):
    ns, _, tq2, _ = qq_ref.shape
    tq = tq2 // 2
    assert tq & (tq - 1) == 0
    cm = _cumsum_matrix()
    cm2 = jnp.concatenate([cm, cm], axis=0)
    pair = lambda p: slice(p * LANES, (p + 1) * LANES)
    zs = []
    for s in range(ns):
        for p in range(HEAD_PAIRS):
            if keys_on_lanes:
                zs.append(_dot(qq_ref[s, p], kblks[s][pair(p), :]))
            else:
                zs.append(_dot_nt(qq_ref[s, p], kblks[s][:, pair(p)]))
    z = jnp.concatenate(zs, axis=0)
    soft = jnp.log(1.0 + jnp.exp2(jnp.abs(z) * -LOG2_E))
    log_beta = jnp.minimum(z, 0.0) - soft
    log_not = log_beta - z
    if diag:
        key = lax.broadcasted_iota(jnp.int32, z.shape, 1)
        query = lax.broadcasted_iota(jnp.int32, z.shape, 0) & (tq - 1)
        before = key < query
        log_not = jnp.where(before, log_not, 0.0)
    hi = log_not.astype(BF16)
    lo = (log_not - hi.astype(F32)).astype(BF16)
    sums = _dot(jnp.concatenate([hi, lo], axis=1), cm2)
    car = car_ref[...]
    wgt = jnp.exp(log_beta + sums[:, :KEY_BLOCK] + car)
    if diag:
        wgt = jnp.where(before, wgt, 0.0)
    car_ref[...] = car + sums[:, KEY_BLOCK:]
    wb = wgt.astype(BF16)
    low_half = lax.broadcasted_iota(jnp.int32, (tq, LANES), 1) < SB_HEAD_DIM
    for s in range(ns):
        for p in range(HEAD_PAIRS):
            r0 = (s * HEAD_PAIRS + p) * 2 * tq
            w2 = wb[r0:r0 + 2 * tq, :]
            pv = _dot_nt(w2, vblks[s][pair(p), :]) if keys_on_lanes else _dot(w2, vblks[s][:, pair(p)])
            acc_ref[s, p] += jnp.where(low_half, pv[0:tq, :], pv[tq:2 * tq, :])


def _sb_all_faded(car_ref):
    return jnp.max(car_ref[...]) < EXP_ZERO_BELOW


def _sb_write(o_ref, acc_ref):
    for s in range(acc_ref.shape[0]):
        for p in range(HEAD_PAIRS):
            o_ref[s, :, p * LANES:(p + 1) * LANES] = acc_ref[s, p].astype(o_ref.dtype)


def _sb_scratch(ns, tq):
    return [pltpu.VMEM((ns, HEAD_PAIRS, 2 * tq, LANES), BF16),
            pltpu.VMEM((ns, HEAD_PAIRS, tq, LANES), F32),
            pltpu.VMEM((ns * N_SB_HEADS * tq, LANES), F32)]


def _sb_kernel(q_ref, k0_ref, v0_ref, pk_hbm, pv_hbm, o_ref, qq_ref, acc_ref, car_ref,
               kbuf, vbuf, sem, *, fresh):
    ns = q_ref.shape[0]
    b0 = pl.program_id(0) * ns
    if fresh:
        newest = pl.program_id(1) - 1
        past_block = lambda hbm, s, blk: hbm.at[b0 + s, blk]
    else:
        newest = pk_hbm.shape[2] // KEY_BLOCK - 1
        past_block = lambda hbm, s, blk: hbm.at[b0 + s, :, pl.ds(pl.multiple_of(blk * KEY_BLOCK, KEY_BLOCK),
                                                                 KEY_BLOCK)]

    def copies(blk):
        slot = blk & 1
        return [pltpu.make_async_copy(past_block(hbm, s, blk), buf.at[slot, s], sem.at[a, slot, s])
                for a, (hbm, buf) in enumerate(((pk_hbm, kbuf), (pv_hbm, vbuf))) for s in range(ns)]

    def start(blk):
        for c in copies(blk):
            c.start()

    def wait(blk):
        for c in copies(blk):
            c.wait()

    @pl.when(newest >= 0)
    def _():
        start(newest)

    _sb_begin(q_ref, qq_ref, acc_ref, car_ref)
    if fresh:
        first_k, first_v = [k0_ref[s, 0] for s in range(ns)], [v0_ref[s, 0] for s in range(ns)]
    else:
        first_k, first_v = [k0_ref[s] for s in range(ns)], [v0_ref[s] for s in range(ns)]
    _sb_tile(qq_ref, first_k, first_v, acc_ref, car_ref, diag=True, keys_on_lanes=fresh)

    def more(state):
        blk, faded = state
        return jnp.logical_and(blk >= 0, jnp.logical_not(faded))

    def step(state):
        blk, _ = state
        wait(blk)

        @pl.when(blk >= 1)
        def _():
            start(blk - 1)

        slot = blk & 1
        _sb_tile(qq_ref, [kbuf[slot, s].astype(BF16) for s in range(ns)],
                 [vbuf[slot, s].astype(BF16) for s in range(ns)], acc_ref, car_ref,
                 diag=False, keys_on_lanes=True)
        return blk - 1, _sb_all_faded(car_ref)

    blk, _ = lax.while_loop(more, step, (newest, _sb_all_faded(car_ref)))

    @pl.when(blk >= 0)
    def _():
        wait(blk)

    _sb_write(o_ref, acc_ref)


def _sb_attention(q, k0, v0, past_k, past_v, ns, fresh):
    batch, seq, _ = q.shape
    assert batch % ns == 0
    hbm = pl.BlockSpec(memory_space=pl.ANY)
    if fresh:
        assert seq % KEY_BLOCK == 0
        tq, grid = KEY_BLOCK, (batch // ns, seq // KEY_BLOCK)
        qspec = pl.BlockSpec((ns, tq, SB_WIDTH), lambda b, i: (b, i, 0))
        nspec = pl.BlockSpec((ns, 1, SB_WIDTH, KEY_BLOCK), lambda b, i: (b, i, 0, 0))
    else:
        past = past_k.shape[2]
        assert seq <= KEY_BLOCK and seq % 16 == 0 and past % KEY_BLOCK == 0 and past > 0
        pad = ((0, 0), (0, KEY_BLOCK - seq), (0, 0))
        k0, v0 = jnp.pad(k0, pad), jnp.pad(v0, pad)
        tq, grid = seq, (batch // ns,)
        qspec = pl.BlockSpec((ns, tq, SB_WIDTH), lambda b: (b, 0, 0))
        nspec = pl.BlockSpec((ns, KEY_BLOCK, SB_WIDTH), lambda b: (b, 0, 0))
    stage = pltpu.VMEM((2, ns, SB_WIDTH, KEY_BLOCK), past_k.dtype)
    return pl.pallas_call(
        functools.partial(_sb_kernel, fresh=fresh),
        grid=grid,
        in_specs=[qspec, nspec, nspec, hbm, hbm],
        out_specs=qspec,
        out_shape=jax.ShapeDtypeStruct(q.shape, BF16),
        scratch_shapes=_sb_scratch(ns, tq) + [stage, stage, pltpu.SemaphoreType.DMA((2, 2, ns))],
        compiler_params=_cparams(len(grid)),
        name="sb_prompt" if fresh else "sb_sample",
    )(q, k0, v0, past_k, past_v)


def _lru_kernel(u_ref, g_ref, h0_ref, c0_ref, cw_ref, cb_ref, gw_ref, rb_ref, ib_ref, lam_ref,
                o_ref, hl_ref, cn_ref, h_ref, ext_ref, a_ref, x_ref, hseq_ref):
    i = pl.program_id(1)
    nb, tc, d = u_ref.shape
    n = nb * tc
    hist = SUBLANES

    @pl.when(i == 0)
    def _():
        h_ref[...] = h0_ref[...]
        ext_ref[:, 0:hist, :] = c0_ref[...]

    u3 = u_ref[...]
    ext_ref[:, hist:, :] = u3
    uc = cb_ref[...] + u3.reshape(n, d) * cw_ref[CONV_W - 1:CONV_W, :]
    for back in range(1, CONV_W):
        prev = ext_ref[:, hist - back:hist - back + tc, :].reshape(n, d)
        uc = uc + prev * cw_ref[CONV_W - 1 - back:CONV_W - back, :]
    tail = ext_ref[:, tc:tc + hist, :]
    ext_ref[:, 0:hist, :] = tail

    ucb = uc.astype(BF16)
    gates = [_dot(ucb[:, p * LANES:(p + 1) * LANES], gw_ref[p]) for p in range(d // LANES)]
    tr = jnp.tanh(jnp.concatenate([gt[:, :LANES] for gt in gates], axis=1) + rb_ref[...])
    ti = jnp.tanh(jnp.concatenate([gt[:, LANES:] for gt in gates], axis=1) + ib_ref[...])
    lam = lam_ref[...]
    softplus_neg = jnp.maximum(-lam, 0.0) + jnp.log1p(jnp.exp(-jnp.abs(lam)))
    log_a = (tr + 1.0) * ((-0.5 * LRU_C) * softplus_neg)
    a = jnp.exp(log_a)
    th = jnp.tanh(log_a)
    p = -2.0 * th
    mult = p * lax.rsqrt(jnp.maximum(p * (1.0 - th), jnp.finfo(F32).tiny))
    x_in = (mult * uc) * (0.5 * ti + 0.5)
    nlb = d // LANES
    pitch = a_ref.shape[1] // nb
    for c in range(nlb):
        for b in range(nb):
            dst = slice(b * pitch, b * pitch + tc)
            a_ref[c, dst, :] = a[b * tc:(b + 1) * tc, c * LANES:(c + 1) * LANES]
            x_ref[c, dst, :] = x_in[b * tc:(b + 1) * tc, c * LANES:(c + 1) * LANES]

    hs = [h_ref[:, c * LANES:(c + 1) * LANES] for c in range(nlb)]
    for t in range(tc):
        step = pl.ds(t, nb, stride=pitch)
        for c in range(nlb):
            hs[c] = a_ref[c, step, :] * hs[c] + x_ref[c, step, :]
            hseq_ref[c, step, :] = hs[c]
    h = jnp.concatenate(hs, axis=1)
    h_ref[...] = h
    h_all = jnp.concatenate(
        [jnp.concatenate([hseq_ref[c, b * pitch:b * pitch + tc, :] for b in range(nb)], axis=0)
         for c in range(nlb)], axis=1)

    g = g_ref[...].reshape(n, d)
    inner = g * (0.7978845608028654 + (0.7978845608028654 * 0.044715) * (g * g))
    o_ref[...] = ((h_all * g) * (0.5 * jnp.tanh(inner) + 0.5)).astype(o_ref.dtype).reshape(nb, tc, d)

    @pl.when(i == pl.num_programs(1) - 1)
    def _():
        hl_ref[...] = h
        cn_ref[...] = tail


def _lru(u, g, h0, conv0, conv_w, conv_b, gate_w, rg_b, ig_b, lam, tc):
    batch, seq, d_rnn = u.shape
    nb = SUBLANES
    assert batch % nb == 0 and seq % tc == 0 and tc % 16 == 0 and seq >= CONV_W - 1
    conv0p = jnp.pad(conv0, ((0, 0), (SUBLANES - (CONV_W - 1), 0), (0, 0)))
    chunk = pl.BlockSpec((nb, tc, d_rnn), lambda b, i: (b, i, 0))
    state = pl.BlockSpec((nb, d_rnn), lambda b, i: (b, 0))
    tails = pl.BlockSpec((nb, SUBLANES, d_rnn), lambda b, i: (b, 0, 0))
    vec = _resident((1, d_rnn))
    pitch = tc + SUBLANES if (tc // SUBLANES) % 2 == 0 else tc
    scan_buf = pltpu.VMEM((d_rnn // LANES, nb * pitch, LANES), F32)
    out, h_last, conv_new = pl.pallas_call(
        _lru_kernel,
        grid=(batch // nb, seq // tc),
        in_specs=[chunk, chunk, state, tails, _resident(conv_w.shape), vec,
                  _resident(gate_w.shape), vec, vec, vec],
        out_specs=[chunk, state, tails],
        out_shape=[jax.ShapeDtypeStruct(u.shape, BF16),
                   jax.ShapeDtypeStruct((batch, d_rnn), F32),
                   jax.ShapeDtypeStruct((batch, SUBLANES, d_rnn), F32)],
        scratch_shapes=[pltpu.VMEM((nb, d_rnn), F32),
                        pltpu.VMEM((nb, SUBLANES + tc, d_rnn), F32),
                        scan_buf, scan_buf, scan_buf],
        compiler_params=_cparams(2),
        name="rg_lru",
    )(u, g, h0, conv0p, conv_w, conv_b, gate_w, rg_b, ig_b, lam)
    return out, h_last, conv_new[:, SUBLANES - (CONV_W - 1):, :]


def _ffn(xb, wg_ref, wu_ref, wd_ref, chunk):
    d_ff = wg_ref.shape[1]
    acc = None
    for c in range(0, d_ff, chunk):
        gate = _dot(xb, wg_ref[:, c:c + chunk])
        up = _dot(xb, wu_ref[:, c:c + chunk])
        hid = (gate * jax.nn.sigmoid(gate) * up).astype(BF16)
        part = _dot(hid, wd_ref[c:c + chunk, :])
        acc = part if acc is None else acc + part
    return acc


def _ffn_chunk(d_ff):
    for chunk in (512, 256, 128):
        if d_ff % chunk == 0:
            return chunk
    return d_ff


def _mix_ffn_kernel(a_ref, l_ref, x_ref, wo_ref, gain_ref, wg_ref, wu_ref, wd_ref, o_ref, *, chunk):
    sbw = a_ref.shape[1]
    y = _dot(a_ref[...], wo_ref[0:sbw, :]) + _dot(l_ref[...], wo_ref[sbw:, :])
    x1 = x_ref[...] + y
    xb = _rms(x1, gain_ref[...]).astype(BF16)
    o_ref[...] = x1 + _ffn(xb, wg_ref, wu_ref, wd_ref, chunk)


def _mix_ffn(attn, lru_out, x2d, w_out, gain, wg, wu, wd, layer, tm):
    rows, d = x2d.shape
    row = lambda width: pl.BlockSpec((tm, width), lambda i: (i, 0))
    return pl.pallas_call(
        functools.partial(_mix_ffn_kernel, chunk=_ffn_chunk(wg.shape[2])),
        grid=(rows // tm,),
        in_specs=[row(attn.shape[1]), row(lru_out.shape[1]), row(d), _resident(w_out.shape),
                  _resident((1, d)), _layer_of(wg, layer), _layer_of(wu, layer), _layer_of(wd, layer)],
        out_specs=row(d),
        out_shape=jax.ShapeDtypeStruct((rows, d), F32),
        compiler_params=_cparams(1),
        name="mix_ffn",
    )(attn, lru_out, x2d, w_out, gain, wg, wu, wd)


def _pool_delta(ext, n, pos0):
    d = ext.shape[1]
    gw = d // len(POOL_WINDOWS)
    pos = pos0 + lax.broadcasted_iota(jnp.int32, (n, 1), 0)
    cur = ext[POOL_HIST:, :]
    sums, s, col0 = [], ext, 0
    for w in POOL_WINDOWS:
        half = w // 2
        s = s[half:, col0:] + s[:-half, col0:]
        first = POOL_HIST - (w - 1)
        sums.append(s[first:first + n, :gw])
        col0 = gw
    outs = []
    for g, w in enumerate(POOL_WINDOWS):
        inv = 1.0 / jnp.minimum(w, pos + 1).astype(F32)
        outs.append(sums[g] * inv - cur[:, g * gw:(g + 1) * gw])
    return outs


def _pool_ffn_kernel(x_ref, buf_ref, gm_ref, pw_ref, ps_ref, gf_ref, wg_ref, wu_ref, wd_ref,
                     gl_ref, o_ref, pn_ref, hist_ref, d_ref, *, chunk, segs_per_batch, start_pos):
    nb, seg, d = x_ref.shape
    gw = d // len(POOL_WINDOWS)
    i = pl.program_id(0)
    x = x_ref[...].reshape(nb * seg, d)
    xn = _rms(x, gm_ref[...])
    if segs_per_batch == 1:
        seg_idx = 0
    else:
        seg_idx = i % segs_per_batch

        @pl.when(seg_idx == 0)
        def _():
            hist_ref[...] = buf_ref[0]

    for b in range(nb):
        hist = buf_ref[b] if segs_per_batch == 1 else hist_ref[...]
        ext = jnp.concatenate([hist, xn[b * seg:(b + 1) * seg, :]], axis=0)
        for g, dg in enumerate(_pool_delta(ext, seg, start_pos + seg_idx * seg)):
            d_ref[b * seg:(b + 1) * seg, g * gw:(g + 1) * gw] = dg.astype(BF16)
        tail = ext[seg:, :]
        if segs_per_batch == 1:
            pn_ref[b] = tail
        else:
            hist_ref[...] = tail

            @pl.when(seg_idx == segs_per_batch - 1)
            def _():
                pn_ref[0] = tail

    y = jnp.concatenate([_dot(d_ref[:, g * gw:(g + 1) * gw], pw_ref[g])
                         for g in range(len(POOL_WINDOWS))], axis=1)
    x1 = x + y * ps_ref[...]
    xb = _rms(x1, gf_ref[...]).astype(BF16)
    x2 = x1 + _ffn(xb, wg_ref, wu_ref, wd_ref, chunk)
    o_ref[...] = _rms(x2, gl_ref[...]).reshape(nb, seg, d)


def _pool_ffn(x2d, pool0, start_pos, gain_mix, pool_w, pool_scale, gain_ffn, wg, wu, wd, layer,
              gain_final, batch, seq, seg, nb):
    d = x2d.shape[1]
    spb = seq // seg
    assert seq % seg == 0 and seg % SUBLANES == 0 and (nb == 1 or spb == 1) and batch % nb == 0
    nseg = batch * spb
    buf = jnp.pad(pool0, ((0, 0), (1, 0), (0, 0)))
    xspec = pl.BlockSpec((nb, seg, d), lambda i: (i, 0, 0))
    bspec = pl.BlockSpec((nb, POOL_HIST, d), (lambda i: (i, 0, 0)) if spb == 1 else (lambda i: (i // spb, 0, 0)))
    vec = _resident((1, d))
    y, pool_new = pl.pallas_call(
        functools.partial(_pool_ffn_kernel, chunk=_ffn_chunk(wg.shape[2]), segs_per_batch=spb,
                          start_pos=start_pos),
        grid=(nseg // nb,),
        in_specs=[xspec, bspec, vec, _resident(pool_w.shape), vec, vec, _layer_of(wg, layer),
                  _layer_of(wu, layer), _layer_of(wd, layer), vec],
        out_specs=[xspec, bspec],
        out_shape=[jax.ShapeDtypeStruct((nseg, seg, d), F32),
                   jax.ShapeDtypeStruct((batch, POOL_HIST, d), F32)],
        scratch_shapes=[pltpu.VMEM((POOL_HIST, d), F32), pltpu.VMEM((nb * seg, d), BF16)],
        compiler_params=_cparams(1),
        name="pool_ffn",
    )(x2d.reshape(nseg, seg, d), buf, gain_mix, pool_w, pool_scale, gain_ffn, wg, wu, wd, gain_final)
    return y.reshape(batch * seq, d), pool_new[:, 1:, :]


def _pick(n, candidates):
    for c in candidates:
        if n % c == 0:
            return c
    return n


def _gate_pairs(rg_w, ig_w):
    nblk, c, _ = rg_w.shape
    z = jnp.zeros((c, c), rg_w.dtype)

    def pair(w, p):
        return jnp.block([[w[2 * p], z], [z, w[2 * p + 1]]])

    return jnp.stack([jnp.concatenate([pair(rg_w, p), pair(ig_w, p)], axis=1)
                      for p in range(nblk // 2)]).astype(BF16)


def _trunk(x, past_k, past_v, h0, conv0, pool0, start_pos, w):
    batch, seq, d = x.shape
    rows = batch * seq
    x2d = x.reshape(rows, d)
    big = rows >= 4096 and seq % 1024 == 0
    tm = 1024 if big else _pick(rows, (512, 256, 128, 64, 32, 16, 8))
    fresh = past_k is None

    q, k, v, kb, vb, u, g = _in_proj(x2d, w["norm_mix"][0:1], w["wq"], w["wkv_t"] if fresh else w["wkv"],
                                     w["wug"], tm, batch, seq, keys_on_lanes=fresh)
    q3 = q.reshape(batch, seq, SB_WIDTH)
    ns = _pick(batch, (8, 4, 2, 1))
    if fresh:
        attn = _sb_attention(q3, kb, vb, kb, vb, ns, fresh=True)
    else:
        attn = _sb_attention(q3, kb.reshape(q3.shape), vb.reshape(q3.shape), past_k, past_v, ns, fresh=False)
    attn = attn.reshape(rows, SB_WIDTH)
    d_rnn = u.shape[1]
    lru_out, h_last, conv_new = _lru(u.reshape(batch, seq, d_rnn), g.reshape(batch, seq, d_rnn), h0, conv0,
                                     w["conv_w"], w["conv_b"], w["gate_w"], w["rg_b"], w["ig_b"], w["lam"],
                                     _pick(seq, (128, 64, 32, 16)))
    lru_out = lru_out.reshape(rows, d_rnn)
    x1 = _mix_ffn(attn, lru_out, x2d, w["w_out"], w["norm_ffn"][0:1], w["ffn_gate"], w["ffn_up"],
                  w["ffn_down"], 0, tm)
    if seq >= 512:
        seg, nb = 1024 if big else _pick(seq, (512, 256, 128)), 1
    else:
        seg, nb = seq, _pick(batch, (8, 4, 2, 1))
    y, pool_new = _pool_ffn(x1, pool0, start_pos, w["norm_mix"][1:2], w["pool_w"], w["pool_scale"],
                            w["norm_ffn"][1:2], w["ffn_gate"], w["ffn_up"], w["ffn_down"], 1,
                            w["norm_final"], batch, seq, seg, nb)
    heads = (N_SB_HEADS, SB_HEAD_DIM)
    if fresh:
        k, v = (t.reshape(1, batch, *heads, seq).transpose(0, 1, 4, 2, 3) for t in (k, v))
    else:
        k, v = (t.reshape(1, batch, seq, *heads) for t in (k, v))
    return (y.reshape(batch, seq, d), k, v, h_last[None], conv_new[None], pool_new[None])


def kernel(x_prompt, x_sample, cache_sb_k, cache_sb_v, state_lru_h, state_lru_conv, state_pool,
           hyb_w_in, hyb_conv_w, hyb_conv_b, hyb_rg_w, hyb_rg_b, hyb_ig_w, hyb_ig_b, hyb_lambda,
           hyb_w_out, pool_w, pool_scale, norm_mix, norm_ffn, ffn_gate, ffn_up, ffn_down, norm_final):
    assert norm_mix.shape[0] == 2 and hyb_w_in.shape[0] == 1 and pool_w.shape[0] == 1
    d = x_prompt.shape[-1]
    d_rnn = hyb_lambda.shape[-1]
    w_in = hyb_w_in[0].astype(BF16)
    w = {
        "wq": w_in[:, 0:SB_WIDTH],
        "wkv": w_in[:, SB_WIDTH:3 * SB_WIDTH],
        "wkv_t": w_in[:, SB_WIDTH:3 * SB_WIDTH].T,
        "wug": w_in[:, 3 * SB_WIDTH:],
        "conv_w": hyb_conv_w[0],
        "conv_b": hyb_conv_b[0].reshape(1, d_rnn),
        "gate_w": _gate_pairs(0.5 * hyb_rg_w[0], 0.5 * hyb_ig_w[0]),
        "rg_b": 0.5 * hyb_rg_b[0].reshape(1, d_rnn),
        "ig_b": 0.5 * hyb_ig_b[0].reshape(1, d_rnn),
        "lam": hyb_lambda[0].reshape(1, d_rnn),
        "w_out": hyb_w_out[0].astype(BF16),
        "pool_w": pool_w[0].astype(BF16),
        "pool_scale": pool_scale[0].reshape(1, d),
        "norm_mix": norm_mix,
        "norm_ffn": norm_ffn,
        "ffn_gate": ffn_gate.astype(BF16),
        "ffn_up": ffn_up.astype(BF16),
        "ffn_down": ffn_down.astype(BF16),
        "norm_final": norm_final.reshape(1, d),
    }
    b = x_prompt.shape[0]
    dt = x_prompt.dtype
    y_p, k_p, v_p, h_p, conv_p, pool_p = _trunk(
        x_prompt, None, None, jnp.zeros((b, d_rnn), dt), jnp.zeros((b, CONV_W - 1, d_rnn), dt),
        jnp.zeros((b, POOL_HIST - 1, d), dt), 0, w)
    db, past = cache_sb_k.shape[1], cache_sb_k.shape[2]
    past_k, past_v = (c[0].transpose(0, 2, 3, 1).reshape(db, SB_WIDTH, past) for c in (cache_sb_k, cache_sb_v))
    y_s, k_s, v_s, h_s, conv_s, pool_s = _trunk(
        x_sample, past_k, past_v, state_lru_h[0], state_lru_conv[0], state_pool[0], past, w)
    return (y_p, y_s, k_p, v_p, h_p, conv_p, pool_p, k_s, v_s, h_s, conv_s, pool_s)
```

```python
import functools

import jax
import jax.numpy as jnp
from jax import lax
from jax.experimental import pallas as pl
from jax.experimental.pallas import tpu as pltpu

F32 = jnp.float32
BF16 = jnp.bfloat16

N_SB_HEADS = 8
SB_HEAD_DIM = 64
SB_WIDTH = N_SB_HEADS * SB_HEAD_DIM
SB_SCALE = SB_HEAD_DIM ** -0.5
N_LRU_BLOCKS = 8
CONV_W = 4
LRU_C = 8.0
POOL_WINDOWS = (2, 4, 8, 16)
POOL_HIST = 16
EPS = 1e-6
LOG2_E = 1.4426950408889634

LANES = 128
SUBLANES = 8
HEAD_PAIRS = SB_WIDTH // LANES
KEY_BLOCK = LANES
EXP_ZERO_BELOW = -104.0
VMEM_LIMIT_BYTES = 56 * 1024 * 1024


def _cparams(n_axes):
    return pltpu.CompilerParams(dimension_semantics=("arbitrary",) * n_axes,
                                vmem_limit_bytes=VMEM_LIMIT_BYTES)


def _resident(shape):
    zeros = (0,) * len(shape)
    return pl.BlockSpec(shape, lambda *_: zeros, pipeline_mode=pl.Buffered(1))


def _layer_of(stack, layer):
    return pl.BlockSpec((None,) + stack.shape[1:], lambda *_: (layer, 0, 0), pipeline_mode=pl.Buffered(1))


def _rms(x, gain):
    ms = jnp.mean(x * x, axis=-1, keepdims=True)
    return (x * lax.rsqrt(ms + EPS)) * gain


def _dot(a, b):
    return jnp.dot(a, b, preferred_element_type=F32)


def _dot_nt(a, b):
    return lax.dot_general(a, b, (((1,), (1,)), ((), ())), preferred_element_type=F32)


def _in_proj_kernel(x_ref, gain_ref, wq_ref, wkv_ref, wug_ref, q_ref, k_ref, v_ref, kb_ref, vb_ref,
                    u_ref, g_ref, *, keys_on_lanes):
    xb = _rms(x_ref[...], gain_ref[...]).astype(BF16)
    w = SB_WIDTH
    q_ref[...] = (_dot(xb, wq_ref[...]) * SB_SCALE).astype(BF16)
    if keys_on_lanes:
        for dst, dst_b, rows in ((k_ref, kb_ref, slice(0, w)), (v_ref, vb_ref, slice(w, 2 * w))):
            t = _dot_nt(wkv_ref[rows, :], xb)
            dst[...] = t
            for blk in range(dst_b.shape[0]):
                dst_b[blk] = t[:, blk * KEY_BLOCK:(blk + 1) * KEY_BLOCK].astype(BF16)
    else:
        for dst, dst_b, cols in ((k_ref, kb_ref, slice(0, w)), (v_ref, vb_ref, slice(w, 2 * w))):
            t = _dot(xb, wkv_ref[:, cols])
            dst[...] = t
            dst_b[...] = t.astype(BF16)
    d_rnn = u_ref.shape[-1]
    u_ref[...] = _dot(xb, wug_ref[:, 0:d_rnn])
    g_ref[...] = _dot(xb, wug_ref[:, d_rnn:2 * d_rnn])


def _in_proj(x2d, gain, wq, wkv, wug, tm, batch, seq, keys_on_lanes):
    rows, d = x2d.shape
    d_rnn = wug.shape[1] // 2
    row = lambda width: pl.BlockSpec((tm, width), lambda i: (i, 0))
    if keys_on_lanes:
        assert seq % tm == 0 and tm % KEY_BLOCK == 0
        tpb, kpt = seq // tm, tm // KEY_BLOCK
        kv = pl.BlockSpec((None, SB_WIDTH, tm), lambda i: (i // tpb, 0, i % tpb))
        kvb = pl.BlockSpec((None, kpt, SB_WIDTH, KEY_BLOCK), lambda i: (i // tpb, i % tpb, 0, 0))
        kv_shape = jax.ShapeDtypeStruct((batch, SB_WIDTH, seq), F32)
        kvb_shape = jax.ShapeDtypeStruct((batch, seq // KEY_BLOCK, SB_WIDTH, KEY_BLOCK), BF16)
    else:
        kv = kvb = row(SB_WIDTH)
        kv_shape = jax.ShapeDtypeStruct((rows, SB_WIDTH), F32)
        kvb_shape = jax.ShapeDtypeStruct((rows, SB_WIDTH), BF16)
    return pl.pallas_call(
        functools.partial(_in_proj_kernel, keys_on_lanes=keys_on_lanes),
        grid=(rows // tm,),
        in_specs=[row(d), _resident((1, d)), _resident(wq.shape), _resident(wkv.shape), _resident(wug.shape)],
        out_specs=[row(SB_WIDTH), kv, kv, kvb, kvb, row(d_rnn), row(d_rnn)],
        out_shape=[jax.ShapeDtypeStruct((rows, SB_WIDTH), BF16), kv_shape, kv_shape, kvb_shape, kvb_shape,
                   jax.ShapeDtypeStruct((rows, d_rnn), F32), jax.ShapeDtypeStruct((rows, d_rnn), F32)],
        compiler_params=_cparams(1),
        name="in_proj",
    )(x2d, gain, wq, wkv, wug)


def _cumsum_matrix():
    s = lax.broadcasted_iota(jnp.int32, (KEY_BLOCK, 2 * KEY_BLOCK), 0)
    j = lax.broadcasted_iota(jnp.int32, (KEY_BLOCK, 2 * KEY_BLOCK), 1)
    return ((s > j) | (j >= KEY_BLOCK)).astype(BF16)


def _sb_begin(q_ref, qq_ref, acc_ref, car_ref):
    ns, tq, _ = q_ref.shape
    acc_ref[...] = jnp.zeros_like(acc_ref)
    car_ref[...] = jnp.zeros_like(car_ref)
    low_half = lax.broadcasted_iota(jnp.int32, (tq, LANES), 1) < SB_HEAD_DIM
    for s in range(ns):
        for p in range(HEAD_PAIRS):
            q2 = q_ref[s, :, p * LANES:(p + 1) * LANES]
            zero = jnp.zeros_like(q2)
            qq_ref[s, p, 0:tq, :] = jnp.where(low_half, q2, zero)
            qq_ref[s, p, tq:2 * tq, :] = jnp.where(low_half, zero, q2)


def _sb_tile(qq_ref, kblks, vblks, acc_ref, car_ref, *, diag, keys_on_lanes):
    ns, _, tq2, _ = qq_ref.shape
    tq = tq2 // 2
    assert tq & (tq - 1) == 0
    cm = _cumsum_matrix()
    cm2 = jnp.concatenate([cm, cm], axis=0)
    pair = lambda p: slice(p * LANES, (p + 1) * LANES)
    zs = []
    for s in range(ns):
        for p in range(HEAD_PAIRS):
            if keys_on_lanes:
                zs.append(_dot(qq_ref[s, p], kblks[s][pair(p), :]))
            else:
                zs.append(_dot_nt(qq_ref[s, p], kblks[s][:, pair(p)]))
    z = jnp.concatenate(zs, axis=0)
    soft = jnp.log(1.0 + jnp.exp2(jnp.abs(z) * -LOG2_E))
    log_beta = jnp.minimum(z, 0.0) - soft
    log_not = log_beta - z
    if diag:
        key = lax.broadcasted_iota(jnp.int32, z.shape, 1)
        query = lax.broadcasted_iota(jnp.int32, z.shape, 0) & (tq - 1)
        before = key < query
        log_not = jnp.where(before, log_not, 0.0)
    hi = log_not.astype(BF16)
    lo = (log_not - hi.astype(F32)).astype(BF16)
    sums = _dot(jnp.concatenate([hi, lo], axis=1), cm2)
    car = car_ref[...]
    wgt = jnp.exp(log_beta + sums[:, :KEY_BLOCK] + car)
    if diag:
        wgt = jnp.where(before, wgt, 0.0)
    car_ref[...] = car + sums[:, KEY_BLOCK:]
    wb = wgt.astype(BF16)
    low_half = lax.broadcasted_iota(jnp.int32, (tq, LANES), 1) < SB_HEAD_DIM
    for s in range(ns):
        for p in range(HEAD_PAIRS):
            r0 = (s * HEAD_PAIRS + p) * 2 * tq
            w2 = wb[r0:r0 + 2 * tq, :]
            pv = _dot_nt(w2, vblks[s][pair(p), :]) if keys_on_lanes else _dot(w2, vblks[s][:, pair(p)])
            acc_ref[s, p] += jnp.where(low_half, pv[0:tq, :], pv[tq:2 * tq, :])


def _sb_all_faded(car_ref):
    return jnp.max(car_ref[...]) < EXP_ZERO_BELOW


def _sb_write(o_ref, acc_ref):
    for s in range(acc_ref.shape[0]):
        for p in range(HEAD_PAIRS):
            o_ref[s, :, p * LANES:(p + 1) * LANES] = acc_ref[s, p].astype(o_ref.dtype)


def _sb_scratch(ns, tq):
    return [pltpu.VMEM((ns, HEAD_PAIRS, 2 * tq, LANES), BF16),
            pltpu.VMEM((ns, HEAD_PAIRS, tq, LANES), F32),
            pltpu.VMEM((ns * N_SB_HEADS * tq, LANES), F32)]


def _sb_kernel(q_ref, k0_ref, v0_ref, pk_hbm, pv_hbm, o_ref, qq_ref, acc_ref, car_ref,
               kbuf, vbuf, sem, *, fresh):
    ns = q_ref.shape[0]
    b0 = pl.program_id(0) * ns
    if fresh:
        newest = pl.program_id(1) - 1
        past_block = lambda hbm, s, blk: hbm.at[b0 + s, blk]
    else:
        newest = pk_hbm.shape[2] // KEY_BLOCK - 1
        past_block = lambda hbm, s, blk: hbm.at[b0 + s, :, pl.ds(pl.multiple_of(blk * KEY_BLOCK, KEY_BLOCK),
                                                                 KEY_BLOCK)]

    def copies(blk):
        slot = blk & 1
        return [pltpu.make_async_copy(past_block(hbm, s, blk), buf.at[slot, s], sem.at[a, slot, s])
                for a, (hbm, buf) in enumerate(((pk_hbm, kbuf), (pv_hbm, vbuf))) for s in range(ns)]

    def start(blk):
        for c in copies(blk):
            c.start()

    def wait(blk):
        for c in copies(blk):
            c.wait()

    @pl.when(newest >= 0)
    def _():
        start(newest)

    _sb_begin(q_ref, qq_ref, acc_ref, car_ref)
    if fresh:
        first_k, first_v = [k0_ref[s, 0] for s in range(ns)], [v0_ref[s, 0] for s in range(ns)]
    else:
        first_k, first_v = [k0_ref[s] for s in range(ns)], [v0_ref[s] for s in range(ns)]
    _sb_tile(qq_ref, first_k, first_v, acc_ref, car_ref, diag=True, keys_on_lanes=fresh)

    def more(state):
        blk, faded = state
        return jnp.logical_and(blk >= 0, jnp.logical_not(faded))

    def step(state):
        blk, _ = state
        wait(blk)

        @pl.when(blk >= 1)
        def _():
            start(blk - 1)

        slot = blk & 1
        _sb_tile(qq_ref, [kbuf[slot, s].astype(BF16) for s in range(ns)],
                 [vbuf[slot, s].astype(BF16) for s in range(ns)], acc_ref, car_ref,
                 diag=False, keys_on_lanes=True)
        return blk - 1, _sb_all_faded(car_ref)

    blk, _ = lax.while_loop(more, step, (newest, _sb_all_faded(car_ref)))

    @pl.when(blk >= 0)
    def _():
        wait(blk)

    _sb_write(o_ref, acc_ref)


def _sb_attention(q, k0, v0, past_k, past_v, ns, fresh):
    batch, seq, _ = q.shape
    assert batch % ns == 0
    hbm = pl.BlockSpec(memory_space=pl.ANY)
    if fresh:
        assert seq % KEY_BLOCK == 0
        tq, grid = KEY_BLOCK, (batch // ns, seq // KEY_BLOCK)
        qspec = pl.BlockSpec((ns, tq, SB_WIDTH), lambda b, i: (b, i, 0))
        nspec = pl.BlockSpec((ns, 1, SB_WIDTH, KEY_BLOCK), lambda b, i: (b, i, 0, 0))
    else:
        past = past_k.shape[2]
        assert seq <= KEY_BLOCK and seq % 16 == 0 and past % KEY_BLOCK == 0 and past > 0
        pad = ((0, 0), (0, KEY_BLOCK - seq), (0, 0))
        k0, v0 = jnp.pad(k0, pad), jnp.pad(v0, pad)
        tq, grid = seq, (batch // ns,)
        qspec = pl.BlockSpec((ns, tq, SB_WIDTH), lambda b: (b, 0, 0))
        nspec = pl.BlockSpec((ns, KEY_BLOCK, SB_WIDTH), lambda b: (b, 0, 0))
    stage = pltpu.VMEM((2, ns, SB_WIDTH, KEY_BLOCK), past_k.dtype)
    return pl.pallas_call(
        functools.partial(_sb_kernel, fresh=fresh),
        grid=grid,
        in_specs=[qspec, nspec, nspec, hbm, hbm],
        out_specs=qspec,
        out_shape=jax.ShapeDtypeStruct(q.shape, BF16),
        scratch_shapes=_sb_scratch(ns, tq) + [stage, stage, pltpu.SemaphoreType.DMA((2, 2, ns))],
        compiler_params=_cparams(len(grid)),
        name="sb_prompt" if fresh else "sb_sample",
    )(q, k0, v0, past_k, past_v)


def _lru_kernel(u_ref, g_ref, h0_ref, c0_ref, cw_ref, cb_ref, gw_ref, rb_ref, ib_ref, lam_ref,
                o_ref, hl_ref, cn_ref, h_ref, ext_ref, a_ref, x_ref, hseq_ref):
    i = pl.program_id(1)
    nb, tc, d = u_ref.shape
    n = nb * tc
    hist = SUBLANES

    @pl.when(i == 0)
    def _():
        h_ref[...] = h0_ref[...]
        ext_ref[:, 0:hist, :] = c0_ref[...]

    u3 = u_ref[...]
    ext_ref[:, hist:, :] = u3
    uc = cb_ref[...] + u3.reshape(n, d) * cw_ref[CONV_W - 1:CONV_W, :]
    for back in range(1, CONV_W):
        prev = ext_ref[:, hist - back:hist - back + tc, :].reshape(n, d)
        uc = uc + prev * cw_ref[CONV_W - 1 - back:CONV_W - back, :]
    tail = ext_ref[:, tc:tc + hist, :]
    ext_ref[:, 0:hist, :] = tail

    ucb = uc.astype(BF16)
    gates = [_dot(ucb[:, p * LANES:(p + 1) * LANES], gw_ref[p]) for p in range(d // LANES)]
    tr = jnp.tanh(jnp.concatenate([gt[:, :LANES] for gt in gates], axis=1) + rb_ref[...])
    ti = jnp.tanh(jnp.concatenate([gt[:, LANES:] for gt in gates], axis=1) + ib_ref[...])
    lam = lam_ref[...]
    softplus_neg = jnp.maximum(-lam, 0.0) + jnp.log1p(jnp.exp(-jnp.abs(lam)))
    log_a = (tr + 1.0) * ((-0.5 * LRU_C) * softplus_neg)
    a = jnp.exp(log_a)
    th = jnp.tanh(log_a)
    p = -2.0 * th
    mult = p * lax.rsqrt(jnp.maximum(p * (1.0 - th), jnp.finfo(F32).tiny))
    x_in = (mult * uc) * (0.5 * ti + 0.5)
    nlb = d // LANES
    pitch = a_ref.shape[1] // nb
    for c in range(nlb):
        for b in range(nb):
            dst = slice(b * pitch, b * pitch + tc)
            a_ref[c, dst, :] = a[b * tc:(b + 1) * tc, c * LANES:(c + 1) * LANES]
            x_ref[c, dst, :] = x_in[b * tc:(b + 1) * tc, c * LANES:(c + 1) * LANES]

    hs = [h_ref[:, c * LANES:(c + 1) * LANES] for c in range(nlb)]
    for t in range(tc):
        step = pl.ds(t, nb, stride=pitch)
        for c in range(nlb):
            hs[c] = a_ref[c, step, :] * hs[c] + x_ref[c, step, :]
            hseq_ref[c, step, :] = hs[c]
    h = jnp.concatenate(hs, axis=1)
    h_ref[...] = h
    h_all = jnp.concatenate(
        [jnp.concatenate([hseq_ref[c, b * pitch:b * pitch + tc, :] for b in range(nb)], axis=0)
         for c in range(nlb)], axis=1)

    g = g_ref[...].reshape(n, d)
    inner = g * (0.7978845608028654 + (0.7978845608028654 * 0.044715) * (g * g))
    o_ref[...] = ((h_all * g) * (0.5 * jnp.tanh(inner) + 0.5)).astype(o_ref.dtype).reshape(nb, tc, d)

    @pl.when(i == pl.num_programs(1) - 1)
    def _():
        hl_ref[...] = h
        cn_ref[...] = tail


def _lru(u, g, h0, conv0, conv_w, conv_b, gate_w, rg_b, ig_b, lam, tc):
    batch, seq, d_rnn = u.shape
    nb = SUBLANES
    assert batch % nb == 0 and seq % tc == 0 and tc % 16 == 0 and seq >= CONV_W - 1
    conv0p = jnp.pad(conv0, ((0, 0), (SUBLANES - (CONV_W - 1), 0), (0, 0)))
    chunk = pl.BlockSpec((nb, tc, d_rnn), lambda b, i: (b, i, 0))
    state = pl.BlockSpec((nb, d_rnn), lambda b, i: (b, 0))
    tails = pl.BlockSpec((nb, SUBLANES, d_rnn), lambda b, i: (b, 0, 0))
    vec = _resident((1, d_rnn))
    pitch = tc + SUBLANES if (tc // SUBLANES) % 2 == 0 else tc
    scan_buf = pltpu.VMEM((d_rnn // LANES, nb * pitch, LANES), F32)
    out, h_last, conv_new = pl.pallas_call(
        _lru_kernel,
        grid=(batch // nb, seq // tc),
        in_specs=[chunk, chunk, state, tails, _resident(conv_w.shape), vec,
                  _resident(gate_w.shape), vec, vec, vec],
        out_specs=[chunk, state, tails],
        out_shape=[jax.ShapeDtypeStruct(u.shape, BF16),
                   jax.ShapeDtypeStruct((batch, d_rnn), F32),
                   jax.ShapeDtypeStruct((batch, SUBLANES, d_rnn), F32)],
        scratch_shapes=[pltpu.VMEM((nb, d_rnn), F32),
                        pltpu.VMEM((nb, SUBLANES + tc, d_rnn), F32),
                        scan_buf, scan_buf, scan_buf],
        compiler_params=_cparams(2),
        name="rg_lru",
    )(u, g, h0, conv0p, conv_w, conv_b, gate_w, rg_b, ig_b, lam)
    return out, h_last, conv_new[:, SUBLANES - (CONV_W - 1):, :]


def _after(x, anchor):
    bits = lax.bitcast_convert_type(anchor[0:1, 0:LANES], jnp.uint32)
    zero = lax.shift_right_logical(lax.shift_right_logical(bits, jnp.uint32(16)), jnp.uint32(16)).astype(F32)
    return x + jnp.concatenate([zero] * (x.shape[1] // LANES), axis=1)


def _ffn(xb, wg_ref, wu_ref, wd_ref, chunk, hooks=None):
    d_ff = wg_ref.shape[1]
    acc = None
    for n, c in enumerate(range(0, d_ff, chunk)):
        gate = _dot(xb, wg_ref[:, c:c + chunk])
        up = _dot(xb, wu_ref[:, c:c + chunk])
        hid = (gate * jax.nn.sigmoid(gate) * up).astype(BF16)
        part = _dot(hid, wd_ref[c:c + chunk, :])
        acc = part if acc is None else acc + part
        if hooks and n in hooks:
            hooks[n](part)
    return acc


def _ffn_chunk(d_ff):
    for chunk in (512, 256, 128):
        if d_ff % chunk == 0:
            return chunk
    return d_ff


def _mix_ffn_kernel(a_ref, l_ref, x_ref, wo_ref, gain_ref, wg_ref, wu_ref, wd_ref, o_ref, *, chunk):
    sbw = a_ref.shape[1]
    y = _dot(a_ref[...], wo_ref[0:sbw, :]) + _dot(l_ref[...], wo_ref[sbw:, :])
    x1 = x_ref[...] + y
    xb = _rms(x1, gain_ref[...]).astype(BF16)
    o_ref[...] = x1 + _ffn(xb, wg_ref, wu_ref, wd_ref, chunk)


def _mix_ffn(attn, lru_out, x2d, w_out, gain, wg, wu, wd, layer, tm):
    rows, d = x2d.shape
    row = lambda width: pl.BlockSpec((tm, width), lambda i: (i, 0))
    return pl.pallas_call(
        functools.partial(_mix_ffn_kernel, chunk=_ffn_chunk(wg.shape[2])),
        grid=(rows // tm,),
        in_specs=[row(attn.shape[1]), row(lru_out.shape[1]), row(d), _resident(w_out.shape),
                  _resident((1, d)), _layer_of(wg, layer), _layer_of(wu, layer), _layer_of(wd, layer)],
        out_specs=row(d),
        out_shape=jax.ShapeDtypeStruct((rows, d), F32),
        compiler_params=_cparams(1),
        name="mix_ffn",
    )(attn, lru_out, x2d, w_out, gain, wg, wu, wd)


def _pool_delta(ext, n, pos0):
    d = ext.shape[1]
    gw = d // len(POOL_WINDOWS)
    pos = pos0 + lax.broadcasted_iota(jnp.int32, (n, 1), 0)
    cur = ext[POOL_HIST:, :]
    sums, s, col0 = [], ext, 0
    for w in POOL_WINDOWS:
        half = w // 2
        s = s[half:, col0:] + s[:-half, col0:]
        first = POOL_HIST - (w - 1)
        sums.append(s[first:first + n, :gw])
        col0 = gw
    outs = []
    for g, w in enumerate(POOL_WINDOWS):
        inv = 1.0 / jnp.minimum(w, pos + 1).astype(F32)
        outs.append(sums[g] * inv - cur[:, g * gw:(g + 1) * gw])
    return outs


def _pool_ffn_kernel(x_ref, buf_ref, gm_ref, pw_ref, ps_ref, gf_ref, wg_ref, wu_ref, wd_ref,
                     gl_ref, o_ref, pn_ref, hist_ref, d_ref, *, chunk, segs_per_batch, start_pos):
    nb, seg, d = x_ref.shape
    gw = d // len(POOL_WINDOWS)
    i = pl.program_id(0)
    x = x_ref[...].reshape(nb * seg, d)
    if segs_per_batch == 1:
        seg_idx = 0
    else:
        seg_idx = i % segs_per_batch

        @pl.when(seg_idx == 0)
        def _():
            hist_ref[...] = buf_ref[0]

    def windows(xr, hist, rows, pos):
        n = xr.shape[0]
        ext = jnp.concatenate([hist, _rms(xr, gm_ref[...])], axis=0)
        for g, dg in enumerate(_pool_delta(ext, n, pos)):
            d_ref[rows, g * gw:(g + 1) * gw] = dg.astype(BF16)
        return ext[n:, :]

    def mixed(xr, rows):
        y = jnp.concatenate([_dot(d_ref[rows, g * gw:(g + 1) * gw], pw_ref[g])
                             for g in range(len(POOL_WINDOWS))], axis=1)
        x1 = xr + y * ps_ref[...]
        return x1, _rms(x1, gf_ref[...]).astype(BF16)

    def ffn_out(x1, xb, hooks=None):
        return _rms(x1 + _ffn(xb, wg_ref, wu_ref, wd_ref, chunk, hooks), gl_ref[...])

    pos0 = start_pos + seg_idx * seg
    if nb == 1 and seg % 1024 == 0:
        half = seg // 2
        rows_a, rows_b = slice(0, half), slice(half, seg)
        hist = windows(x[rows_a, :], buf_ref[0] if segs_per_batch == 1 else hist_ref[...], rows_a, pos0)
        x1a, xba = mixed(x[rows_a, :], rows_a)
        second = {}

        def start_b(part):
            second["x"] = _after(x[rows_b, :], part)
            second["hist"] = windows(second["x"], hist, rows_b, pos0 + half)

        def mix_b(part):
            second["x1"], second["xb"] = mixed(second["x"], rows_b)

        n_chunks = wg_ref.shape[1] // chunk
        assert n_chunks >= 2
        o_ref[0, rows_a, :] = ffn_out(x1a, xba, {0: start_b, n_chunks // 2: mix_b})
        o_ref[0, rows_b, :] = ffn_out(second["x1"], second["xb"])
        tails = [second["hist"]]
    else:
        tails = []
        for b in range(nb):
            rows = slice(b * seg, (b + 1) * seg)
            tails.append(windows(x[rows, :], buf_ref[b] if segs_per_batch == 1 else hist_ref[...], rows, pos0))
        o_ref[...] = ffn_out(*mixed(x, slice(0, nb * seg))).reshape(nb, seg, d)

    if segs_per_batch == 1:
        for b in range(nb):
            pn_ref[b] = tails[b]
    else:
        hist_ref[...] = tails[0]

        @pl.when(seg_idx == segs_per_batch - 1)
        def _():
            pn_ref[0] = tails[0]


def _pool_ffn(x2d, pool0, start_pos, gain_mix, pool_w, pool_scale, gain_ffn, wg, wu, wd, layer,
              gain_final, batch, seq, seg, nb):
    d = x2d.shape[1]
    spb = seq // seg
    assert seq % seg == 0 and seg % SUBLANES == 0 and (nb == 1 or spb == 1) and batch % nb == 0
    nseg = batch * spb
    buf = jnp.pad(pool0, ((0, 0), (1, 0), (0, 0)))
    xspec = pl.BlockSpec((nb, seg, d), lambda i: (i, 0, 0))
    bspec = pl.BlockSpec((nb, POOL_HIST, d), (lambda i: (i, 0, 0)) if spb == 1 else (lambda i: (i // spb, 0, 0)))
    vec = _resident((1, d))
    y, pool_new = pl.pallas_call(
        functools.partial(_pool_ffn_kernel, chunk=_ffn_chunk(wg.shape[2]), segs_per_batch=spb,
                          start_pos=start_pos),
        grid=(nseg // nb,),
        in_specs=[xspec, bspec, vec, _resident(pool_w.shape), vec, vec, _layer_of(wg, layer),
                  _layer_of(wu, layer), _layer_of(wd, layer), vec],
        out_specs=[xspec, bspec],
        out_shape=[jax.ShapeDtypeStruct((nseg, seg, d), F32),
                   jax.ShapeDtypeStruct((batch, POOL_HIST, d), F32)],
        scratch_shapes=[pltpu.VMEM((POOL_HIST, d), F32), pltpu.VMEM((nb * seg, d), BF16)],
        compiler_params=_cparams(1),
        name="pool_ffn",
    )(x2d.reshape(nseg, seg, d), buf, gain_mix, pool_w, pool_scale, gain_ffn, wg, wu, wd, gain_final)
    return y.reshape(batch * seq, d), pool_new[:, 1:, :]


def _pick(n, candidates):
    for c in candidates:
        if n % c == 0:
            return c
    return n


def _gate_pairs(rg_w, ig_w):
    nblk, c, _ = rg_w.shape
    z = jnp.zeros((c, c), rg_w.dtype)

    def pair(w, p):
        return jnp.block([[w[2 * p], z], [z, w[2 * p + 1]]])

    return jnp.stack([jnp.concatenate([pair(rg_w, p), pair(ig_w, p)], axis=1)
                      for p in range(nblk // 2)]).astype(BF16)


def _trunk(x, past_k, past_v, h0, conv0, pool0, start_pos, w):
    batch, seq, d = x.shape
    rows = batch * seq
    x2d = x.reshape(rows, d)
    big = rows >= 4096 and seq % 1024 == 0
    tm = 1024 if big else _pick(rows, (512, 256, 128, 64, 32, 16, 8))
    fresh = past_k is None

    q, k, v, kb, vb, u, g = _in_proj(x2d, w["norm_mix"][0:1], w["wq"], w["wkv_t"] if fresh else w["wkv"],
                                     w["wug"], tm, batch, seq, keys_on_lanes=fresh)
    q3 = q.reshape(batch, seq, SB_WIDTH)
    ns = _pick(batch, (8, 4, 2, 1))
    if fresh:
        attn = _sb_attention(q3, kb, vb, kb, vb, ns, fresh=True)
    else:
        attn = _sb_attention(q3, kb.reshape(q3.shape), vb.reshape(q3.shape), past_k, past_v, ns, fresh=False)
    attn = attn.reshape(rows, SB_WIDTH)
    d_rnn = u.shape[1]
    lru_out, h_last, conv_new = _lru(u.reshape(batch, seq, d_rnn), g.reshape(batch, seq, d_rnn), h0, conv0,
                                     w["conv_w"], w["conv_b"], w["gate_w"], w["rg_b"], w["ig_b"], w["lam"],
                                     _pick(seq, (128, 64, 32, 16)))
    lru_out = lru_out.reshape(rows, d_rnn)
    x1 = _mix_ffn(attn, lru_out, x2d, w["w_out"], w["norm_ffn"][0:1], w["ffn_gate"], w["ffn_up"],
                  w["ffn_down"], 0, tm)
    if seq >= 512:
        seg, nb = 1024 if big else _pick(seq, (512, 256, 128)), 1
    else:
        seg, nb = seq, _pick(batch, (8, 4, 2, 1))
    y, pool_new = _pool_ffn(x1, pool0, start_pos, w["norm_mix"][1:2], w["pool_w"], w["pool_scale"],
                            w["norm_ffn"][1:2], w["ffn_gate"], w["ffn_up"], w["ffn_down"], 1,
                            w["norm_final"], batch, seq, seg, nb)
    heads = (N_SB_HEADS, SB_HEAD_DIM)
    if fresh:
        k, v = (t.reshape(1, batch, *heads, seq).transpose(0, 1, 4, 2, 3) for t in (k, v))
    else:
        k, v = (t.reshape(1, batch, seq, *heads) for t in (k, v))
    return (y.reshape(batch, seq, d), k, v, h_last[None], conv_new[None], pool_new[None])


def kernel(x_prompt, x_sample, cache_sb_k, cache_sb_v, state_lru_h, state_lru_conv, state_pool,
           hyb_w_in, hyb_conv_w, hyb_conv_b, hyb_rg_w, hyb_rg_b, hyb_ig_w, hyb_ig_b, hyb_lambda,
           hyb_w_out, pool_w, pool_scale, norm_mix, norm_ffn, ffn_gate, ffn_up, ffn_down, norm_final):
    assert norm_mix.shape[0] == 2 and hyb_w_in.shape[0] == 1 and pool_w.shape[0] == 1
    d = x_prompt.shape[-1]
    d_rnn = hyb_lambda.shape[-1]
    w_in = hyb_w_in[0].astype(BF16)
    w = {
        "wq": w_in[:, 0:SB_WIDTH],
        "wkv": w_in[:, SB_WIDTH:3 * SB_WIDTH],
        "wkv_t": w_in[:, SB_WIDTH:3 * SB_WIDTH].T,
        "wug": w_in[:, 3 * SB_WIDTH:],
        "conv_w": hyb_conv_w[0],
        "conv_b": hyb_conv_b[0].reshape(1, d_rnn),
        "gate_w": _gate_pairs(0.5 * hyb_rg_w[0], 0.5 * hyb_ig_w[0]),
        "rg_b": 0.5 * hyb_rg_b[0].reshape(1, d_rnn),
        "ig_b": 0.5 * hyb_ig_b[0].reshape(1, d_rnn),
        "lam": hyb_lambda[0].reshape(1, d_rnn),
        "w_out": hyb_w_out[0].astype(BF16),
        "pool_w": pool_w[0].astype(BF16),
        "pool_scale": pool_scale[0].reshape(1, d),
        "norm_mix": norm_mix,
        "norm_ffn": norm_ffn,
        "ffn_gate": ffn_gate.astype(BF16),
        "ffn_up": ffn_up.astype(BF16),
        "ffn_down": ffn_down.astype(BF16),
        "norm_final": norm_final.reshape(1, d),
    }
    b = x_prompt.shape[0]
    dt = x_prompt.dtype
    y_p, k_p, v_p, h_p, conv_p, pool_p = _trunk(
        x_prompt, None, None, jnp.zeros((b, d_rnn), dt), jnp.zeros((b, CONV_W - 1, d_rnn), dt),
        jnp.zeros((b, POOL_HIST - 1, d), dt), 0, w)
    db, past = cache_sb_k.shape[1], cache_sb_k.shape[2]
    past_k, past_v = (c[0].transpose(0, 2, 3, 1).reshape(db, SB_WIDTH, past) for c in (cache_sb_k, cache_sb_v))
    y_s, k_s, v_s, h_s, conv_s, pool_s = _trunk(
        x_sample, past_k, past_v, state_lru_h[0], state_lru_conv[0], state_pool[0], past, w)
    return (y_p, y_s, k_p, v_p, h_p, conv_p, pool_p, k_s, v_s, h_s, conv_s, pool_s)
```

```python
import functools

import jax
import jax.numpy as jnp
from jax import lax
from jax.experimental import pallas as pl
from jax.experimental.pallas import tpu as pltpu

F32 = jnp.float32
BF16 = jnp.bfloat16

N_SB_HEADS = 8
SB_HEAD_DIM = 64
SB_WIDTH = N_SB_HEADS * SB_HEAD_DIM
SB_SCALE = SB_HEAD_DIM ** -0.5
CONV_W = 4
LRU_C = 8.0
POOL_WINDOWS = (2, 4, 8, 16)
POOL_HIST = 16
EPS = 1e-6
LOG2_E = 1.4426950408889634
GELU_C = 0.7978845608028654
GELU_CUBIC = 0.044715

LANES = 128
SUBLANES = 8
BF16_ROWS = 16
BIG_ROW_TILE = 1024
HEAD_PAIRS = SB_WIDTH // LANES
KEY_BLOCK = LANES
EXP_ZERO_BELOW = -104.0
VMEM_LIMIT_BYTES = 56 * 1024 * 1024


def _cparams(n_axes):
    return pltpu.CompilerParams(dimension_semantics=("arbitrary",) * n_axes,
                                vmem_limit_bytes=VMEM_LIMIT_BYTES)


def _resident(shape):
    zeros = (0,) * len(shape)
    return pl.BlockSpec(shape, lambda *_: zeros, pipeline_mode=pl.Buffered(1))


def _layer_of(stack, layer):
    return pl.BlockSpec((None,) + stack.shape[1:], lambda *_: (layer, 0, 0), pipeline_mode=pl.Buffered(1))


def _rms(x, gain):
    ms = jnp.mean(x * x, axis=-1, keepdims=True)
    return (x * lax.rsqrt(ms + EPS)) * gain


def _dot(a, b):
    return jnp.dot(a, b, preferred_element_type=F32)


def _dot_nt(a, b):
    return lax.dot_general(a, b, (((1,), (1,)), ((), ())), preferred_element_type=F32)


def _in_proj_kernel(x_ref, gain_ref, wq_ref, wkv_ref, wug_ref, q_ref, k_ref, v_ref, kb_ref, vb_ref,
                    u_ref, g_ref, *, keys_on_lanes):
    xb = _rms(x_ref[...], gain_ref[...]).astype(BF16)
    w = SB_WIDTH
    d_rnn = u_ref.shape[-1]
    g = _dot(xb, wug_ref[:, d_rnn:2 * d_rnn])
    inner = g * (GELU_C + (GELU_C * GELU_CUBIC) * (g * g))
    g_ref[...] = g * (0.5 * jnp.tanh(inner) + 0.5)
    q_ref[...] = (_dot(xb, wq_ref[...]) * SB_SCALE).astype(BF16)
    if keys_on_lanes:
        for dst, dst_b, rows in ((k_ref, kb_ref, slice(0, w)), (v_ref, vb_ref, slice(w, 2 * w))):
            t = _dot_nt(wkv_ref[rows, :], xb)
            dst[...] = t
            for blk in range(dst_b.shape[0]):
                dst_b[blk] = t[:, blk * KEY_BLOCK:(blk + 1) * KEY_BLOCK].astype(BF16)
    else:
        for dst, dst_b, cols in ((k_ref, kb_ref, slice(0, w)), (v_ref, vb_ref, slice(w, 2 * w))):
            t = _dot(xb, wkv_ref[:, cols])
            dst[...] = t
            dst_b[...] = t.astype(BF16)
    u_ref[...] = _dot(xb, wug_ref[:, 0:d_rnn])


def _in_proj(x2d, gain, wq, wkv, wug, tm, batch, seq, keys_on_lanes):
    rows, d = x2d.shape
    d_rnn = wug.shape[1] // 2
    row = lambda width: pl.BlockSpec((tm, width), lambda i: (i, 0))
    if keys_on_lanes:
        assert seq % tm == 0 and tm % KEY_BLOCK == 0
        tpb, kpt = seq // tm, tm // KEY_BLOCK
        kv = pl.BlockSpec((None, SB_WIDTH, tm), lambda i: (i // tpb, 0, i % tpb))
        kvb = pl.BlockSpec((None, kpt, SB_WIDTH, KEY_BLOCK), lambda i: (i // tpb, i % tpb, 0, 0))
        kv_shape = jax.ShapeDtypeStruct((batch, SB_WIDTH, seq), F32)
        kvb_shape = jax.ShapeDtypeStruct((batch, seq // KEY_BLOCK, SB_WIDTH, KEY_BLOCK), BF16)
    else:
        kv = kvb = row(SB_WIDTH)
        kv_shape = jax.ShapeDtypeStruct((rows, SB_WIDTH), F32)
        kvb_shape = jax.ShapeDtypeStruct((rows, SB_WIDTH), BF16)
    return pl.pallas_call(
        functools.partial(_in_proj_kernel, keys_on_lanes=keys_on_lanes),
        grid=(rows // tm,),
        in_specs=[row(d), _resident((1, d)), _resident(wq.shape), _resident(wkv.shape), _resident(wug.shape)],
        out_specs=[row(SB_WIDTH), kv, kv, kvb, kvb, row(d_rnn), row(d_rnn)],
        out_shape=[jax.ShapeDtypeStruct((rows, SB_WIDTH), BF16), kv_shape, kv_shape, kvb_shape, kvb_shape,
                   jax.ShapeDtypeStruct((rows, d_rnn), F32), jax.ShapeDtypeStruct((rows, d_rnn), F32)],
        compiler_params=_cparams(1),
        name="in_proj",
    )(x2d, gain, wq, wkv, wug)


def _cumsum_matrix():
    s = lax.broadcasted_iota(jnp.int32, (KEY_BLOCK, 2 * KEY_BLOCK), 0)
    j = lax.broadcasted_iota(jnp.int32, (KEY_BLOCK, 2 * KEY_BLOCK), 1)
    return ((s > j) | (j >= KEY_BLOCK)).astype(BF16)


def _sb_begin(q_ref, qq_ref, acc_ref, car_ref):
    ns, tq, _ = q_ref.shape
    acc_ref[...] = jnp.zeros_like(acc_ref)
    car_ref[...] = jnp.zeros_like(car_ref)
    low_half = lax.broadcasted_iota(jnp.int32, (tq, LANES), 1) < SB_HEAD_DIM
    for s in range(ns):
        for p in range(HEAD_PAIRS):
            q2 = q_ref[s, :, p * LANES:(p + 1) * LANES]
            zero = jnp.zeros_like(q2)
            qq_ref[s, p, 0:tq, :] = jnp.where(low_half, q2, zero)
            qq_ref[s, p, tq:2 * tq, :] = jnp.where(low_half, zero, q2)


def _sb_tile(qq_ref, kblks, vblks, acc_ref, car_ref, *, diag, keys_on_lanes):
    ns, _, tq2, _ = qq_ref.shape
    tq = tq2 // 2
    assert tq & (tq - 1) == 0
    cm = _cumsum_matrix()
    cm2 = jnp.concatenate([cm, cm], axis=0)
    pair = lambda p: slice(p * LANES, (p + 1) * LANES)
    zs = []
    for s in range(ns):
        for p in range(HEAD_PAIRS):
            if keys_on_lanes:
                zs.append(_dot(qq_ref[s, p], kblks[s][pair(p), :]))
            else:
                zs.append(_dot_nt(qq_ref[s, p], kblks[s][:, pair(p)]))
    z = jnp.concatenate(zs, axis=0)
    soft = jnp.log(1.0 + jnp.exp2(jnp.abs(z) * -LOG2_E))
    log_beta = jnp.minimum(z, 0.0) - soft
    log_not = log_beta - z
    if diag:
        key = lax.broadcasted_iota(jnp.int32, z.shape, 1)
        query = lax.broadcasted_iota(jnp.int32, z.shape, 0) & (tq - 1)
        before = key < query
        log_not = jnp.where(before, log_not, 0.0)
    hi = log_not.astype(BF16)
    lo = (log_not - hi.astype(F32)).astype(BF16)
    sums = _dot(jnp.concatenate([hi, lo], axis=1), cm2)
    car = car_ref[...]
    wgt = jnp.exp(log_beta + sums[:, :KEY_BLOCK] + car)
    if diag:
        wgt = jnp.where(before, wgt, 0.0)
    car_ref[...] = car + sums[:, KEY_BLOCK:]
    wb = wgt.astype(BF16)
    low_half = lax.broadcasted_iota(jnp.int32, (tq, LANES), 1) < SB_HEAD_DIM
    for s in range(ns):
        for p in range(HEAD_PAIRS):
            r0 = (s * HEAD_PAIRS + p) * 2 * tq
            w2 = wb[r0:r0 + 2 * tq, :]
            pv = _dot_nt(w2, vblks[s][pair(p), :]) if keys_on_lanes else _dot(w2, vblks[s][:, pair(p)])
            acc_ref[s, p] += jnp.where(low_half, pv[0:tq, :], pv[tq:2 * tq, :])


def _sb_all_faded(car_ref):
    return jnp.max(car_ref[...]) < EXP_ZERO_BELOW


def _sb_write(o_ref, acc_ref):
    for s in range(acc_ref.shape[0]):
        for p in range(HEAD_PAIRS):
            o_ref[s, :, p * LANES:(p + 1) * LANES] = acc_ref[s, p].astype(o_ref.dtype)


def _sb_scratch(ns, tq):
    return [pltpu.VMEM((ns, HEAD_PAIRS, 2 * tq, LANES), BF16),
            pltpu.VMEM((ns, HEAD_PAIRS, tq, LANES), F32),
            pltpu.VMEM((ns * N_SB_HEADS * tq, LANES), F32)]


def _sb_kernel(q_ref, k0_ref, v0_ref, pk_hbm, pv_hbm, o_ref, qq_ref, acc_ref, car_ref,
               kbuf, vbuf, sem, *, fresh):
    ns = q_ref.shape[0]
    b0 = pl.program_id(0) * ns
    if fresh:
        newest = pl.program_id(1) - 1
        past_block = lambda hbm, s, blk: hbm.at[b0 + s, blk]
    else:
        newest = pk_hbm.shape[2] // KEY_BLOCK - 1
        past_block = lambda hbm, s, blk: hbm.at[b0 + s, :, pl.ds(pl.multiple_of(blk * KEY_BLOCK, KEY_BLOCK),
                                                                 KEY_BLOCK)]

    def copies(blk):
        slot = blk & 1
        return [pltpu.make_async_copy(past_block(hbm, s, blk), buf.at[slot, s], sem.at[a, slot, s])
                for a, (hbm, buf) in enumerate(((pk_hbm, kbuf), (pv_hbm, vbuf))) for s in range(ns)]

    def start(blk):
        for c in copies(blk):
            c.start()

    def wait(blk):
        for c in copies(blk):
            c.wait()

    @pl.when(newest >= 0)
    def _():
        start(newest)

    _sb_begin(q_ref, qq_ref, acc_ref, car_ref)
    if fresh:
        first_k, first_v = [k0_ref[s, 0] for s in range(ns)], [v0_ref[s, 0] for s in range(ns)]
    else:
        first_k, first_v = [k0_ref[s] for s in range(ns)], [v0_ref[s] for s in range(ns)]
    _sb_tile(qq_ref, first_k, first_v, acc_ref, car_ref, diag=True, keys_on_lanes=fresh)

    def more(state):
        blk, faded = state
        return jnp.logical_and(blk >= 0, jnp.logical_not(faded))

    def step(state):
        blk, _ = state
        wait(blk)

        @pl.when(blk >= 1)
        def _():
            start(blk - 1)

        slot = blk & 1
        _sb_tile(qq_ref, [kbuf[slot, s].astype(BF16) for s in range(ns)],
                 [vbuf[slot, s].astype(BF16) for s in range(ns)], acc_ref, car_ref,
                 diag=False, keys_on_lanes=True)
        return blk - 1, _sb_all_faded(car_ref)

    blk, _ = lax.while_loop(more, step, (newest, _sb_all_faded(car_ref)))

    @pl.when(blk >= 0)
    def _():
        wait(blk)

    _sb_write(o_ref, acc_ref)


def _sb_attention(q, k0, v0, past_k, past_v, ns, fresh):
    batch, seq, _ = q.shape
    assert batch % ns == 0
    hbm = pl.BlockSpec(memory_space=pl.ANY)
    if fresh:
        assert seq % KEY_BLOCK == 0
        tq, grid = KEY_BLOCK, (batch // ns, seq // KEY_BLOCK)
        qspec = pl.BlockSpec((ns, tq, SB_WIDTH), lambda b, i: (b, i, 0))
        nspec = pl.BlockSpec((ns, 1, SB_WIDTH, KEY_BLOCK), lambda b, i: (b, i, 0, 0))
    else:
        past = past_k.shape[2]
        assert seq <= KEY_BLOCK and seq % BF16_ROWS == 0 and past % KEY_BLOCK == 0 and past > 0
        pad = ((0, 0), (0, KEY_BLOCK - seq), (0, 0))
        k0, v0 = jnp.pad(k0, pad), jnp.pad(v0, pad)
        tq, grid = seq, (batch // ns,)
        qspec = pl.BlockSpec((ns, tq, SB_WIDTH), lambda b: (b, 0, 0))
        nspec = pl.BlockSpec((ns, KEY_BLOCK, SB_WIDTH), lambda b: (b, 0, 0))
    stage = pltpu.VMEM((2, ns, SB_WIDTH, KEY_BLOCK), past_k.dtype)
    return pl.pallas_call(
        functools.partial(_sb_kernel, fresh=fresh),
        grid=grid,
        in_specs=[qspec, nspec, nspec, hbm, hbm],
        out_specs=qspec,
        out_shape=jax.ShapeDtypeStruct(q.shape, BF16),
        scratch_shapes=_sb_scratch(ns, tq) + [stage, stage, pltpu.SemaphoreType.DMA((2, 2, ns))],
        compiler_params=_cparams(len(grid)),
        name="sb_prompt" if fresh else "sb_sample",
    )(q, k0, v0, past_k, past_v)


def _lru_kernel(u_ref, g_ref, h0_ref, c0_ref, cw_ref, cb_ref, gw_ref, rb_ref, ib_ref, lam_ref,
                o_ref, hl_ref, cn_ref, h_ref, ext_ref, a_ref, x_ref, hseq_ref):
    i = pl.program_id(1)
    nb, tc, d = u_ref.shape
    n = nb * tc
    hist = SUBLANES

    @pl.when(i == 0)
    def _():
        h_ref[...] = h0_ref[...]
        ext_ref[:, 0:hist, :] = c0_ref[...]

    u3 = u_ref[...]
    ext_ref[:, hist:, :] = u3
    uc = cb_ref[...] + u3.reshape(n, d) * cw_ref[CONV_W - 1:CONV_W, :]
    for back in range(1, CONV_W):
        prev = ext_ref[:, hist - back:hist - back + tc, :].reshape(n, d)
        uc = uc + prev * cw_ref[CONV_W - 1 - back:CONV_W - back, :]
    tail = ext_ref[:, tc:tc + hist, :]
    ext_ref[:, 0:hist, :] = tail

    ucb = uc.astype(BF16)
    gates = [_dot(ucb[:, p * LANES:(p + 1) * LANES], gw_ref[p]) for p in range(d // LANES)]
    tr = jnp.tanh(jnp.concatenate([gt[:, :LANES] for gt in gates], axis=1) + rb_ref[...])
    ti = jnp.tanh(jnp.concatenate([gt[:, LANES:] for gt in gates], axis=1) + ib_ref[...])
    lam = lam_ref[...]
    softplus_neg = jnp.maximum(-lam, 0.0) + jnp.log1p(jnp.exp(-jnp.abs(lam)))
    log_a = (tr + 1.0) * ((-0.5 * LRU_C) * softplus_neg)
    a = jnp.exp(log_a)
    th = jnp.tanh(log_a)
    p = -2.0 * th
    mult = p * lax.rsqrt(jnp.maximum(p * (1.0 - th), jnp.finfo(F32).tiny))
    x_in = (mult * uc) * (0.5 * ti + 0.5)
    nlb = d // LANES
    pitch = a_ref.shape[1] // nb
    for c in range(nlb):
        for b in range(nb):
            dst = slice(b * pitch, b * pitch + tc)
            a_ref[c, dst, :] = a[b * tc:(b + 1) * tc, c * LANES:(c + 1) * LANES]
            x_ref[c, dst, :] = x_in[b * tc:(b + 1) * tc, c * LANES:(c + 1) * LANES]

    hs = [h_ref[:, c * LANES:(c + 1) * LANES] for c in range(nlb)]
    for t in range(tc):
        step = pl.ds(t, nb, stride=pitch)
        for c in range(nlb):
            hs[c] = a_ref[c, step, :] * hs[c] + x_ref[c, step, :]
            hseq_ref[c, step, :] = hs[c]
    h = jnp.concatenate(hs, axis=1)
    h_ref[...] = h
    h_all = jnp.concatenate(
        [jnp.concatenate([hseq_ref[c, b * pitch:b * pitch + tc, :] for b in range(nb)], axis=0)
         for c in range(nlb)], axis=1)

    o_ref[...] = (h_all * g_ref[...].reshape(n, d)).astype(o_ref.dtype).reshape(nb, tc, d)

    @pl.when(i == pl.num_programs(1) - 1)
    def _():
        hl_ref[...] = h
        cn_ref[...] = tail


def _lru(u, g, h0, conv0, conv_w, conv_b, gate_w, rg_b, ig_b, lam, tc):
    batch, seq, d_rnn = u.shape
    nb = SUBLANES
    assert batch % nb == 0 and seq % tc == 0 and tc % BF16_ROWS == 0 and seq >= CONV_W - 1
    conv0p = jnp.pad(conv0, ((0, 0), (SUBLANES - (CONV_W - 1), 0), (0, 0)))
    chunk = pl.BlockSpec((nb, tc, d_rnn), lambda b, i: (b, i, 0))
    state = pl.BlockSpec((nb, d_rnn), lambda b, i: (b, 0))
    tails = pl.BlockSpec((nb, SUBLANES, d_rnn), lambda b, i: (b, 0, 0))
    vec = _resident((1, d_rnn))
    pitch = tc + SUBLANES if (tc // SUBLANES) % 2 == 0 else tc
    scan_buf = pltpu.VMEM((d_rnn // LANES, nb * pitch, LANES), F32)
    out, h_last, conv_new = pl.pallas_call(
        _lru_kernel,
        grid=(batch // nb, seq // tc),
        in_specs=[chunk, chunk, state, tails, _resident(conv_w.shape), vec,
                  _resident(gate_w.shape), vec, vec, vec],
        out_specs=[chunk, state, tails],
        out_shape=[jax.ShapeDtypeStruct(u.shape, BF16),
                   jax.ShapeDtypeStruct((batch, d_rnn), F32),
                   jax.ShapeDtypeStruct((batch, SUBLANES, d_rnn), F32)],
        scratch_shapes=[pltpu.VMEM((nb, d_rnn), F32),
                        pltpu.VMEM((nb, SUBLANES + tc, d_rnn), F32),
                        scan_buf, scan_buf, scan_buf],
        compiler_params=_cparams(2),
        name="rg_lru",
    )(u, g, h0, conv0p, conv_w, conv_b, gate_w, rg_b, ig_b, lam)
    return out, h_last, conv_new[:, SUBLANES - (CONV_W - 1):, :]


def _after(x, anchor):
    bits = lax.bitcast_convert_type(anchor[0:1, 0:LANES], jnp.uint32)
    zero = lax.shift_right_logical(lax.shift_right_logical(bits, jnp.uint32(16)), jnp.uint32(16)).astype(F32)
    return x + jnp.concatenate([zero] * (x.shape[1] // LANES), axis=1)


def _ffn(xb, wg_ref, wu_ref, wd_ref, chunk, hooks=None):
    d_ff = wg_ref.shape[1]
    acc = None
    for n, c in enumerate(range(0, d_ff, chunk)):
        gate = _dot(xb, wg_ref[:, c:c + chunk])
        up = _dot(xb, wu_ref[:, c:c + chunk])
        hid = (gate * jax.nn.sigmoid(gate) * up).astype(BF16)
        part = _dot(hid, wd_ref[c:c + chunk, :])
        acc = part if acc is None else acc + part
        if hooks and n in hooks:
            hooks[n](part)
    return acc


def _ffn_chunk(d_ff):
    for chunk in (512, 256, 128):
        if d_ff % chunk == 0:
            return chunk
    return d_ff


def _mix_ffn_kernel(a_ref, l_ref, x_ref, wo_ref, gain_ref, wg_ref, wu_ref, wd_ref, o_ref, *, chunk):
    sbw = a_ref.shape[1]
    y = _dot(a_ref[...], wo_ref[0:sbw, :]) + _dot(l_ref[...], wo_ref[sbw:, :])
    x1 = x_ref[...] + y
    xb = _rms(x1, gain_ref[...]).astype(BF16)
    o_ref[...] = x1 + _ffn(xb, wg_ref, wu_ref, wd_ref, chunk)


def _mix_ffn(attn, lru_out, x2d, w_out, gain, wg, wu, wd, layer, tm):
    rows, d = x2d.shape
    row = lambda width: pl.BlockSpec((tm, width), lambda i: (i, 0))
    return pl.pallas_call(
        functools.partial(_mix_ffn_kernel, chunk=_ffn_chunk(wg.shape[2])),
        grid=(rows // tm,),
        in_specs=[row(attn.shape[1]), row(lru_out.shape[1]), row(d), _resident(w_out.shape),
                  _resident((1, d)), _layer_of(wg, layer), _layer_of(wu, layer), _layer_of(wd, layer)],
        out_specs=row(d),
        out_shape=jax.ShapeDtypeStruct((rows, d), F32),
        compiler_params=_cparams(1),
        name="mix_ffn",
    )(attn, lru_out, x2d, w_out, gain, wg, wu, wd)


def _pool_delta(ext, n, pos0):
    d = ext.shape[1]
    gw = d // len(POOL_WINDOWS)
    pos = pos0 + lax.broadcasted_iota(jnp.int32, (n, 1), 0)
    cur = ext[POOL_HIST:, :]
    sums, s, col0 = [], ext, 0
    for w in POOL_WINDOWS:
        half = w // 2
        s = s[half:, col0:] + s[:-half, col0:]
        first = POOL_HIST - (w - 1)
        sums.append(s[first:first + n, :gw])
        col0 = gw
    outs = []
    for g, w in enumerate(POOL_WINDOWS):
        inv = 1.0 / jnp.minimum(w, pos + 1).astype(F32)
        outs.append(sums[g] * inv - cur[:, g * gw:(g + 1) * gw])
    return outs


def _pool_ffn_kernel(x_ref, buf_ref, gm_ref, pw_ref, ps_ref, gf_ref, wg_ref, wu_ref, wd_ref,
                     gl_ref, o_ref, pn_ref, hist_ref, d_ref, *, chunk, segs_per_batch, start_pos):
    nb, seg, d = x_ref.shape
    gw = d // len(POOL_WINDOWS)
    i = pl.program_id(0)
    x = x_ref[...].reshape(nb * seg, d)
    if segs_per_batch == 1:
        seg_idx = 0
    else:
        seg_idx = i % segs_per_batch

        @pl.when(seg_idx == 0)
        def _():
            hist_ref[...] = buf_ref[0]

    def windows(xr, hist, rows, pos):
        n = xr.shape[0]
        ext = jnp.concatenate([hist, _rms(xr, gm_ref[...])], axis=0)
        for g, dg in enumerate(_pool_delta(ext, n, pos)):
            d_ref[rows, g * gw:(g + 1) * gw] = dg.astype(BF16)
        return ext[n:, :]

    def mixed(xr, rows):
        y = jnp.concatenate([_dot(d_ref[rows, g * gw:(g + 1) * gw], pw_ref[g])
                             for g in range(len(POOL_WINDOWS))], axis=1)
        x1 = xr + y * ps_ref[...]
        return x1, _rms(x1, gf_ref[...]).astype(BF16)

    def ffn_out(x1, xb, hooks=None):
        return _rms(x1 + _ffn(xb, wg_ref, wu_ref, wd_ref, chunk, hooks), gl_ref[...])

    pos0 = start_pos + seg_idx * seg
    if nb == 1 and seg % BIG_ROW_TILE == 0:
        half = seg // 2
        rows_a, rows_b = slice(0, half), slice(half, seg)
        hist = windows(x[rows_a, :], buf_ref[0] if segs_per_batch == 1 else hist_ref[...], rows_a, pos0)
        x1a, xba = mixed(x[rows_a, :], rows_a)
        second = {}

        def start_b(part):
            second["x"] = _after(x[rows_b, :], part)
            second["hist"] = windows(second["x"], hist, rows_b, pos0 + half)

        def mix_b(part):
            second["x1"], second["xb"] = mixed(second["x"], rows_b)

        n_chunks = wg_ref.shape[1] // chunk
        assert n_chunks >= 2
        o_ref[0, rows_a, :] = ffn_out(x1a, xba, {0: start_b, n_chunks // 2: mix_b})
        o_ref[0, rows_b, :] = ffn_out(second["x1"], second["xb"])
        tails = [second["hist"]]
    else:
        tails = []
        for b in range(nb):
            rows = slice(b * seg, (b + 1) * seg)
            tails.append(windows(x[rows, :], buf_ref[b] if segs_per_batch == 1 else hist_ref[...], rows, pos0))
        o_ref[...] = ffn_out(*mixed(x, slice(0, nb * seg))).reshape(nb, seg, d)

    if segs_per_batch == 1:
        for b in range(nb):
            pn_ref[b] = tails[b]
    else:
        hist_ref[...] = tails[0]

        @pl.when(seg_idx == segs_per_batch - 1)
        def _():
            pn_ref[0] = tails[0]


def _pool_ffn(x2d, pool0, start_pos, gain_mix, pool_w, pool_scale, gain_ffn, wg, wu, wd, layer,
              gain_final, batch, seq, seg, nb):
    d = x2d.shape[1]
    spb = seq // seg
    assert seq % seg == 0 and seg % SUBLANES == 0 and (nb == 1 or spb == 1) and batch % nb == 0
    nseg = batch * spb
    buf = jnp.pad(pool0, ((0, 0), (1, 0), (0, 0)))
    xspec = pl.BlockSpec((nb, seg, d), lambda i: (i, 0, 0))
    bspec = pl.BlockSpec((nb, POOL_HIST, d), (lambda i: (i, 0, 0)) if spb == 1 else (lambda i: (i // spb, 0, 0)))
    vec = _resident((1, d))
    y, pool_new = pl.pallas_call(
        functools.partial(_pool_ffn_kernel, chunk=_ffn_chunk(wg.shape[2]), segs_per_batch=spb,
                          start_pos=start_pos),
        grid=(nseg // nb,),
        in_specs=[xspec, bspec, vec, _resident(pool_w.shape), vec, vec, _layer_of(wg, layer),
                  _layer_of(wu, layer), _layer_of(wd, layer), vec],
        out_specs=[xspec, bspec],
        out_shape=[jax.ShapeDtypeStruct((nseg, seg, d), F32),
                   jax.ShapeDtypeStruct((batch, POOL_HIST, d), F32)],
        scratch_shapes=[pltpu.VMEM((POOL_HIST, d), F32), pltpu.VMEM((nb * seg, d), BF16)],
        compiler_params=_cparams(1),
        name="pool_ffn",
    )(x2d.reshape(nseg, seg, d), buf, gain_mix, pool_w, pool_scale, gain_ffn, wg, wu, wd, gain_final)
    return y.reshape(batch * seq, d), pool_new[:, 1:, :]


def _pick(n, candidates):
    for c in candidates:
        if n % c == 0:
            return c
    return n


def _gate_pairs(rg_w, ig_w):
    nblk, c, _ = rg_w.shape
    z = jnp.zeros((c, c), rg_w.dtype)

    def pair(w, p):
        return jnp.block([[w[2 * p], z], [z, w[2 * p + 1]]])

    return jnp.stack([jnp.concatenate([pair(rg_w, p), pair(ig_w, p)], axis=1)
                      for p in range(nblk // 2)]).astype(BF16)


def _trunk(x, past_k, past_v, h0, conv0, pool0, start_pos, w):
    batch, seq, d = x.shape
    rows = batch * seq
    x2d = x.reshape(rows, d)
    big = rows >= 4 * BIG_ROW_TILE and seq % BIG_ROW_TILE == 0
    tm = BIG_ROW_TILE if big else _pick(rows, (512, 256, 128, 64, 32, 16, 8))
    fresh = past_k is None

    q, k, v, kb, vb, u, g = _in_proj(x2d, w["norm_mix"][0:1], w["wq"], w["wkv_t"] if fresh else w["wkv"],
                                     w["wug"], tm, batch, seq, keys_on_lanes=fresh)
    q3 = q.reshape(batch, seq, SB_WIDTH)
    ns = _pick(batch, (8, 4, 2, 1))
    if fresh:
        attn = _sb_attention(q3, kb, vb, kb, vb, ns, fresh=True)
    else:
        attn = _sb_attention(q3, kb.reshape(q3.shape), vb.reshape(q3.shape), past_k, past_v, ns, fresh=False)
    attn = attn.reshape(rows, SB_WIDTH)
    d_rnn = u.shape[1]
    lru_out, h_last, conv_new = _lru(u.reshape(batch, seq, d_rnn), g.reshape(batch, seq, d_rnn), h0, conv0,
                                     w["conv_w"], w["conv_b"], w["gate_w"], w["rg_b"], w["ig_b"], w["lam"],
                                     _pick(seq, (128, 64, 32, 16)))
    lru_out = lru_out.reshape(rows, d_rnn)
    x1 = _mix_ffn(attn, lru_out, x2d, w["w_out"], w["norm_ffn"][0:1], w["ffn_gate"], w["ffn_up"],
                  w["ffn_down"], 0, tm)
    if seq >= 512:
        seg, nb = BIG_ROW_TILE if big else _pick(seq, (512, 256, 128)), 1
    else:
        seg, nb = seq, _pick(batch, (8, 4, 2, 1))
    y, pool_new = _pool_ffn(x1, pool0, start_pos, w["norm_mix"][1:2], w["pool_w"], w["pool_scale"],
                            w["norm_ffn"][1:2], w["ffn_gate"], w["ffn_up"], w["ffn_down"], 1,
                            w["norm_final"], batch, seq, seg, nb)
    heads = (N_SB_HEADS, SB_HEAD_DIM)
    if fresh:
        k, v = (t.reshape(1, batch, *heads, seq).transpose(0, 1, 4, 2, 3) for t in (k, v))
    else:
        k, v = (t.reshape(1, batch, seq, *heads) for t in (k, v))
    return (y.reshape(batch, seq, d), k, v, h_last[None], conv_new[None], pool_new[None])


def kernel(x_prompt, x_sample, cache_sb_k, cache_sb_v, state_lru_h, state_lru_conv, state_pool,
           hyb_w_in, hyb_conv_w, hyb_conv_b, hyb_rg_w, hyb_rg_b, hyb_ig_w, hyb_ig_b, hyb_lambda,
           hyb_w_out, pool_w, pool_scale, norm_mix, norm_ffn, ffn_gate, ffn_up, ffn_down, norm_final):
    assert norm_mix.shape[0] == 2 and hyb_w_in.shape[0] == 1 and pool_w.shape[0] == 1
    d = x_prompt.shape[-1]
    d_rnn = hyb_lambda.shape[-1]
    w_in = hyb_w_in[0].astype(BF16)
    w = {
        "wq": w_in[:, 0:SB_WIDTH],
        "wkv": w_in[:, SB_WIDTH:3 * SB_WIDTH],
        "wkv_t": w_in[:, SB_WIDTH:3 * SB_WIDTH].T,
        "wug": w_in[:, 3 * SB_WIDTH:],
        "conv_w": hyb_conv_w[0],
        "conv_b": hyb_conv_b[0].reshape(1, d_rnn),
        "gate_w": _gate_pairs(0.5 * hyb_rg_w[0], 0.5 * hyb_ig_w[0]),
        "rg_b": 0.5 * hyb_rg_b[0].reshape(1, d_rnn),
        "ig_b": 0.5 * hyb_ig_b[0].reshape(1, d_rnn),
        "lam": hyb_lambda[0].reshape(1, d_rnn),
        "w_out": hyb_w_out[0].astype(BF16),
        "pool_w": pool_w[0].astype(BF16),
        "pool_scale": pool_scale[0].reshape(1, d),
        "norm_mix": norm_mix,
        "norm_ffn": norm_ffn,
        "ffn_gate": ffn_gate.astype(BF16),
        "ffn_up": ffn_up.astype(BF16),
        "ffn_down": ffn_down.astype(BF16),
        "norm_final": norm_final.reshape(1, d),
    }
    b = x_prompt.shape[0]
    dt = x_prompt.dtype
    y_p, k_p, v_p, h_p, conv_p, pool_p = _trunk(
        x_prompt, None, None, jnp.zeros((b, d_rnn), dt), jnp.zeros((b, CONV_W - 1, d_rnn), dt),
        jnp.zeros((b, POOL_HIST - 1, d), dt), 0, w)
    db, past = cache_sb_k.shape[1], cache_sb_k.shape[2]
    past_k, past_v = (c[0].transpose(0, 2, 3, 1).reshape(db, SB_WIDTH, past) for c in (cache_sb_k, cache_sb_v))
    y_s, k_s, v_s, h_s, conv_s, pool_s = _trunk(
        x_sample, past_k, past_v, state_lru_h[0], state_lru_conv[0], state_pool[0], past, w)
    return (y_p, y_s, k_p, v_p, h_p, conv_p, pool_p, k_s, v_s, h_s, conv_s, pool_s)
```

```python
import functools

import jax
import jax.numpy as jnp
from jax import lax
from jax.experimental import pallas as pl
from jax.experimental.pallas import tpu as pltpu

F32 = jnp.float32
BF16 = jnp.bfloat16

N_SB_HEADS = 8
SB_HEAD_DIM = 64
SB_WIDTH = N_SB_HEADS * SB_HEAD_DIM
SB_SCALE = SB_HEAD_DIM ** -0.5
CONV_W = 4
LRU_C = 8.0
POOL_WINDOWS = (2, 4, 8, 16)
POOL_HIST = 16
EPS = 1e-6
LOG2_E = 1.4426950408889634
GELU_C = 0.7978845608028654
GELU_CUBIC = 0.044715

LANES = 128
SUBLANES = 8
BF16_ROWS = 16
BIG_ROW_TILE = 1024
HEAD_PAIRS = SB_WIDTH // LANES
KEY_BLOCK = LANES
EXP_ZERO_BELOW = -104.0
VMEM_LIMIT_BYTES = 56 * 1024 * 1024


def _cparams(n_axes):
    return pltpu.CompilerParams(dimension_semantics=("arbitrary",) * n_axes,
                                vmem_limit_bytes=VMEM_LIMIT_BYTES)


def _resident(shape):
    zeros = (0,) * len(shape)
    return pl.BlockSpec(shape, lambda *_: zeros, pipeline_mode=pl.Buffered(1))


def _layer_of(stack, layer):
    return pl.BlockSpec((None,) + stack.shape[1:], lambda *_: (layer, 0, 0), pipeline_mode=pl.Buffered(1))


def _rms(x, gain):
    ms = jnp.mean(x * x, axis=-1, keepdims=True)
    return (x * lax.rsqrt(ms + EPS)) * gain


def _dot(a, b):
    return jnp.dot(a, b, preferred_element_type=F32)


def _dot_nt(a, b):
    return lax.dot_general(a, b, (((1,), (1,)), ((), ())), preferred_element_type=F32)


def _in_proj_kernel(x_ref, gain_ref, wq_ref, wkv_ref, wug_ref, q_ref, k_ref, v_ref, kb_ref, vb_ref,
                    u_ref, g_ref, *, keys_on_lanes):
    xb = _rms(x_ref[...], gain_ref[...]).astype(BF16)
    w = SB_WIDTH
    d_rnn = u_ref.shape[-1]
    g = _dot(xb, wug_ref[:, d_rnn:2 * d_rnn])
    inner = g * (GELU_C + (GELU_C * GELU_CUBIC) * (g * g))
    g_ref[...] = g * (0.5 * jnp.tanh(inner) + 0.5)
    q_ref[...] = (_dot(xb, wq_ref[...]) * SB_SCALE).astype(BF16)
    if keys_on_lanes:
        for dst, dst_b, rows in ((k_ref, kb_ref, slice(0, w)), (v_ref, vb_ref, slice(w, 2 * w))):
            t = _dot_nt(wkv_ref[rows, :], xb)
            dst[...] = t
            for blk in range(dst_b.shape[0]):
                dst_b[blk] = t[:, blk * KEY_BLOCK:(blk + 1) * KEY_BLOCK].astype(BF16)
    else:
        for dst, dst_b, cols in ((k_ref, kb_ref, slice(0, w)), (v_ref, vb_ref, slice(w, 2 * w))):
            t = _dot(xb, wkv_ref[:, cols])
            dst[...] = t
            dst_b[...] = t.astype(BF16)
    u_ref[...] = _dot(xb, wug_ref[:, 0:d_rnn])


def _in_proj(x2d, gain, wq, wkv, wug, tm, batch, seq, keys_on_lanes):
    rows, d = x2d.shape
    d_rnn = wug.shape[1] // 2
    row = lambda width: pl.BlockSpec((tm, width), lambda i: (i, 0))
    if keys_on_lanes:
        assert seq % tm == 0 and tm % KEY_BLOCK == 0
        tpb, kpt = seq // tm, tm // KEY_BLOCK
        kv = pl.BlockSpec((None, SB_WIDTH, tm), lambda i: (i // tpb, 0, i % tpb))
        kvb = pl.BlockSpec((None, kpt, SB_WIDTH, KEY_BLOCK), lambda i: (i // tpb, i % tpb, 0, 0))
        kv_shape = jax.ShapeDtypeStruct((batch, SB_WIDTH, seq), F32)
        kvb_shape = jax.ShapeDtypeStruct((batch, seq // KEY_BLOCK, SB_WIDTH, KEY_BLOCK), BF16)
    else:
        kv = kvb = row(SB_WIDTH)
        kv_shape = jax.ShapeDtypeStruct((rows, SB_WIDTH), F32)
        kvb_shape = jax.ShapeDtypeStruct((rows, SB_WIDTH), BF16)
    return pl.pallas_call(
        functools.partial(_in_proj_kernel, keys_on_lanes=keys_on_lanes),
        grid=(rows // tm,),
        in_specs=[row(d), _resident((1, d)), _resident(wq.shape), _resident(wkv.shape), _resident(wug.shape)],
        out_specs=[row(SB_WIDTH), kv, kv, kvb, kvb, row(d_rnn), row(d_rnn)],
        out_shape=[jax.ShapeDtypeStruct((rows, SB_WIDTH), BF16), kv_shape, kv_shape, kvb_shape, kvb_shape,
                   jax.ShapeDtypeStruct((rows, d_rnn), F32), jax.ShapeDtypeStruct((rows, d_rnn), F32)],
        compiler_params=_cparams(1),
        name="in_proj",
    )(x2d, gain, wq, wkv, wug)


def _cumsum_matrix():
    s = lax.broadcasted_iota(jnp.int32, (KEY_BLOCK, 2 * KEY_BLOCK), 0)
    j = lax.broadcasted_iota(jnp.int32, (KEY_BLOCK, 2 * KEY_BLOCK), 1)
    return ((s > j) | (j >= KEY_BLOCK)).astype(BF16)


def _sb_begin(q_ref, qq_ref, acc_ref, car_ref):
    ns, tq, _ = q_ref.shape
    acc_ref[...] = jnp.zeros_like(acc_ref)
    car_ref[...] = jnp.zeros_like(car_ref)
    low_half = lax.broadcasted_iota(jnp.int32, (tq, LANES), 1) < SB_HEAD_DIM
    for s in range(ns):
        for p in range(HEAD_PAIRS):
            q2 = q_ref[s, :, p * LANES:(p + 1) * LANES]
            zero = jnp.zeros_like(q2)
            qq_ref[s, p, 0:tq, :] = jnp.where(low_half, q2, zero)
            qq_ref[s, p, tq:2 * tq, :] = jnp.where(low_half, zero, q2)


def _sb_tile(qq_ref, kblks, vblks, acc_ref, car_ref, *, diag, keys_on_lanes):
    ns, _, tq2, _ = qq_ref.shape
    tq = tq2 // 2
    assert tq & (tq - 1) == 0
    cm = _cumsum_matrix()
    cm2 = jnp.concatenate([cm, cm], axis=0)
    pair = lambda p: slice(p * LANES, (p + 1) * LANES)
    zs = []
    for s in range(ns):
        for p in range(HEAD_PAIRS):
            if keys_on_lanes:
                zs.append(_dot(qq_ref[s, p], kblks[s][pair(p), :]))
            else:
                zs.append(_dot_nt(qq_ref[s, p], kblks[s][:, pair(p)]))
    z = jnp.concatenate(zs, axis=0)
    soft = jnp.log(1.0 + jnp.exp2(jnp.abs(z) * -LOG2_E))
    log_beta = jnp.minimum(z, 0.0) - soft
    log_not = log_beta - z
    if diag:
        key = lax.broadcasted_iota(jnp.int32, z.shape, 1)
        query = lax.broadcasted_iota(jnp.int32, z.shape, 0) & (tq - 1)
        before = key < query
        log_not = jnp.where(before, log_not, 0.0)
    hi = log_not.astype(BF16)
    lo = (log_not - hi.astype(F32)).astype(BF16)
    sums = _dot(jnp.concatenate([hi, lo], axis=1), cm2)
    car = car_ref[...]
    wgt = jnp.exp(log_beta + sums[:, :KEY_BLOCK] + car)
    if diag:
        wgt = jnp.where(before, wgt, 0.0)
    car_ref[...] = car + sums[:, KEY_BLOCK:]
    wb = wgt.astype(BF16)
    low_half = lax.broadcasted_iota(jnp.int32, (tq, LANES), 1) < SB_HEAD_DIM
    for s in range(ns):
        for p in range(HEAD_PAIRS):
            r0 = (s * HEAD_PAIRS + p) * 2 * tq
            w2 = wb[r0:r0 + 2 * tq, :]
            pv = _dot_nt(w2, vblks[s][pair(p), :]) if keys_on_lanes else _dot(w2, vblks[s][:, pair(p)])
            acc_ref[s, p] += jnp.where(low_half, pv[0:tq, :], pv[tq:2 * tq, :])


def _sb_all_faded(car_ref):
    return jnp.max(car_ref[...]) < EXP_ZERO_BELOW


def _sb_write(o_ref, acc_ref):
    for s in range(acc_ref.shape[0]):
        for p in range(HEAD_PAIRS):
            o_ref[s, :, p * LANES:(p + 1) * LANES] = acc_ref[s, p].astype(o_ref.dtype)


def _sb_scratch(ns, tq):
    return [pltpu.VMEM((ns, HEAD_PAIRS, 2 * tq, LANES), BF16),
            pltpu.VMEM((ns, HEAD_PAIRS, tq, LANES), F32),
            pltpu.VMEM((ns * N_SB_HEADS * tq, LANES), F32)]


def _sb_kernel(q_ref, k0_ref, v0_ref, pk_hbm, pv_hbm, o_ref, qq_ref, acc_ref, car_ref,
               kbuf, vbuf, sem, *, fresh):
    ns = q_ref.shape[0]
    b0 = pl.program_id(0) * ns
    if fresh:
        newest = pl.program_id(1) - 1
        past_block = lambda hbm, s, blk: hbm.at[b0 + s, blk]
    else:
        newest = pk_hbm.shape[2] // KEY_BLOCK - 1
        past_block = lambda hbm, s, blk: hbm.at[b0 + s, :, pl.ds(pl.multiple_of(blk * KEY_BLOCK, KEY_BLOCK),
                                                                 KEY_BLOCK)]

    def copies(blk):
        slot = blk & 1
        return [pltpu.make_async_copy(past_block(hbm, s, blk), buf.at[slot, s], sem.at[a, slot, s])
                for a, (hbm, buf) in enumerate(((pk_hbm, kbuf), (pv_hbm, vbuf))) for s in range(ns)]

    def start(blk):
        for c in copies(blk):
            c.start()

    def wait(blk):
        for c in copies(blk):
            c.wait()

    @pl.when(newest >= 0)
    def _():
        start(newest)

    _sb_begin(q_ref, qq_ref, acc_ref, car_ref)
    if fresh:
        first_k, first_v = [k0_ref[s, 0] for s in range(ns)], [v0_ref[s, 0] for s in range(ns)]
    else:
        first_k, first_v = [k0_ref[s] for s in range(ns)], [v0_ref[s] for s in range(ns)]
    _sb_tile(qq_ref, first_k, first_v, acc_ref, car_ref, diag=True, keys_on_lanes=fresh)

    def more(state):
        blk, faded = state
        return jnp.logical_and(blk >= 0, jnp.logical_not(faded))

    def step(state):
        blk, _ = state
        wait(blk)

        @pl.when(blk >= 1)
        def _():
            start(blk - 1)

        slot = blk & 1
        _sb_tile(qq_ref, [kbuf[slot, s].astype(BF16) for s in range(ns)],
                 [vbuf[slot, s].astype(BF16) for s in range(ns)], acc_ref, car_ref,
                 diag=False, keys_on_lanes=True)
        return blk - 1, _sb_all_faded(car_ref)

    blk, _ = lax.while_loop(more, step, (newest, _sb_all_faded(car_ref)))

    @pl.when(blk >= 0)
    def _():
        wait(blk)

    _sb_write(o_ref, acc_ref)


def _sb_attention(q, k0, v0, past_k, past_v, ns, fresh):
    batch, seq, _ = q.shape
    assert batch % ns == 0
    hbm = pl.BlockSpec(memory_space=pl.ANY)
    if fresh:
        assert seq % KEY_BLOCK == 0
        tq, grid = KEY_BLOCK, (batch // ns, seq // KEY_BLOCK)
        qspec = pl.BlockSpec((ns, tq, SB_WIDTH), lambda b, i: (b, i, 0))
        nspec = pl.BlockSpec((ns, 1, SB_WIDTH, KEY_BLOCK), lambda b, i: (b, i, 0, 0))
    else:
        past = past_k.shape[2]
        assert seq <= KEY_BLOCK and seq % BF16_ROWS == 0 and past % KEY_BLOCK == 0 and past > 0
        pad = ((0, 0), (0, KEY_BLOCK - seq), (0, 0))
        k0, v0 = jnp.pad(k0, pad), jnp.pad(v0, pad)
        tq, grid = seq, (batch // ns,)
        qspec = pl.BlockSpec((ns, tq, SB_WIDTH), lambda b: (b, 0, 0))
        nspec = pl.BlockSpec((ns, KEY_BLOCK, SB_WIDTH), lambda b: (b, 0, 0))
    stage = pltpu.VMEM((2, ns, SB_WIDTH, KEY_BLOCK), past_k.dtype)
    return pl.pallas_call(
        functools.partial(_sb_kernel, fresh=fresh),
        grid=grid,
        in_specs=[qspec, nspec, nspec, hbm, hbm],
        out_specs=qspec,
        out_shape=jax.ShapeDtypeStruct(q.shape, BF16),
        scratch_shapes=_sb_scratch(ns, tq) + [stage, stage, pltpu.SemaphoreType.DMA((2, 2, ns))],
        compiler_params=_cparams(len(grid)),
        name="sb_prompt" if fresh else "sb_sample",
    )(q, k0, v0, past_k, past_v)


def _lru_kernel(u_ref, g_ref, h0_ref, c0_ref, cw_ref, cb_ref, gw_ref, rb_ref, ib_ref, lam_ref,
                o_ref, hl_ref, cn_ref, h_ref, ext_ref, a_ref, x_ref, hseq_ref):
    i = pl.program_id(1)
    nb, tc, d = u_ref.shape
    n = nb * tc
    hist = SUBLANES

    @pl.when(i == 0)
    def _():
        h_ref[...] = h0_ref[...]
        ext_ref[:, 0:hist, :] = c0_ref[...]

    u3 = u_ref[...]
    ext_ref[:, hist:, :] = u3
    uc = cb_ref[...] + u3.reshape(n, d) * cw_ref[CONV_W - 1:CONV_W, :]
    for back in range(1, CONV_W):
        prev = ext_ref[:, hist - back:hist - back + tc, :].reshape(n, d)
        uc = uc + prev * cw_ref[CONV_W - 1 - back:CONV_W - back, :]
    tail = ext_ref[:, tc:tc + hist, :]
    ext_ref[:, 0:hist, :] = tail

    ucb = uc.astype(BF16)
    gates = [_dot(ucb[:, p * LANES:(p + 1) * LANES], gw_ref[p]) for p in range(d // LANES)]
    tr = jnp.tanh(jnp.concatenate([gt[:, :LANES] for gt in gates], axis=1) + rb_ref[...])
    ti = jnp.tanh(jnp.concatenate([gt[:, LANES:] for gt in gates], axis=1) + ib_ref[...])
    lam = lam_ref[...]
    softplus_neg = jnp.maximum(-lam, 0.0) + jnp.log1p(jnp.exp(-jnp.abs(lam)))
    log_a = (tr + 1.0) * ((-0.5 * LRU_C) * softplus_neg)
    a = jnp.exp(log_a)
    th = jnp.tanh(log_a)
    p = -2.0 * th
    mult = p * lax.rsqrt(jnp.maximum(p * (1.0 - th), jnp.finfo(F32).tiny))
    x_in = (mult * uc) * (0.5 * ti + 0.5)
    nlb = d // LANES
    pitch = a_ref.shape[1] // nb
    for c in range(nlb):
        for b in range(nb):
            dst = slice(b * pitch, b * pitch + tc)
            a_ref[c, dst, :] = a[b * tc:(b + 1) * tc, c * LANES:(c + 1) * LANES]
            x_ref[c, dst, :] = x_in[b * tc:(b + 1) * tc, c * LANES:(c + 1) * LANES]

    hs = [h_ref[:, c * LANES:(c + 1) * LANES] for c in range(nlb)]
    for t in range(tc):
        step = pl.ds(t, nb, stride=pitch)
        for c in range(nlb):
            hs[c] = a_ref[c, step, :] * hs[c] + x_ref[c, step, :]
            hseq_ref[c, step, :] = hs[c]
    h = jnp.concatenate(hs, axis=1)
    h_ref[...] = h
    h_all = jnp.concatenate(
        [jnp.concatenate([hseq_ref[c, b * pitch:b * pitch + tc, :] for b in range(nb)], axis=0)
         for c in range(nlb)], axis=1)

    o_ref[...] = (h_all * g_ref[...].reshape(n, d)).astype(o_ref.dtype).reshape(nb, tc, d)

    @pl.when(i == pl.num_programs(1) - 1)
    def _():
        hl_ref[...] = h
        cn_ref[...] = tail


def _lru(u, g, h0, conv0, conv_w, conv_b, gate_w, rg_b, ig_b, lam, tc):
    batch, seq, d_rnn = u.shape
    nb = SUBLANES
    assert batch % nb == 0 and seq % tc == 0 and tc % BF16_ROWS == 0 and seq >= CONV_W - 1
    conv0p = jnp.pad(conv0, ((0, 0), (SUBLANES - (CONV_W - 1), 0), (0, 0)))
    chunk = pl.BlockSpec((nb, tc, d_rnn), lambda b, i: (b, i, 0))
    state = pl.BlockSpec((nb, d_rnn), lambda b, i: (b, 0))
    tails = pl.BlockSpec((nb, SUBLANES, d_rnn), lambda b, i: (b, 0, 0))
    vec = _resident((1, d_rnn))
    pitch = tc + SUBLANES if (tc // SUBLANES) % 2 == 0 else tc
    scan_buf = pltpu.VMEM((d_rnn // LANES, nb * pitch, LANES), F32)
    out, h_last, conv_new = pl.pallas_call(
        _lru_kernel,
        grid=(batch // nb, seq // tc),
        in_specs=[chunk, chunk, state, tails, _resident(conv_w.shape), vec,
                  _resident(gate_w.shape), vec, vec, vec],
        out_specs=[chunk, state, tails],
        out_shape=[jax.ShapeDtypeStruct(u.shape, BF16),
                   jax.ShapeDtypeStruct((batch, d_rnn), F32),
                   jax.ShapeDtypeStruct((batch, SUBLANES, d_rnn), F32)],
        scratch_shapes=[pltpu.VMEM((nb, d_rnn), F32),
                        pltpu.VMEM((nb, SUBLANES + tc, d_rnn), F32),
                        scan_buf, scan_buf, scan_buf],
        compiler_params=_cparams(2),
        name="rg_lru",
    )(u, g, h0, conv0p, conv_w, conv_b, gate_w, rg_b, ig_b, lam)
    return out, h_last, conv_new[:, SUBLANES - (CONV_W - 1):, :]


def _after(x, anchor):
    bits = lax.bitcast_convert_type(anchor[0:1, 0:LANES], jnp.uint32)
    zero = lax.shift_right_logical(lax.shift_right_logical(bits, jnp.uint32(16)), jnp.uint32(16)).astype(F32)
    return x + jnp.concatenate([zero] * (x.shape[1] // LANES), axis=1)


def _ffn(xb, wg_ref, wu_ref, wd_ref, chunk, hooks=None):
    d_ff = wg_ref.shape[1]
    acc = None
    for n, c in enumerate(range(0, d_ff, chunk)):
        gate = _dot(xb, wg_ref[:, c:c + chunk])
        up = _dot(xb, wu_ref[:, c:c + chunk])
        hid = (gate * jax.nn.sigmoid(gate) * up).astype(BF16)
        part = _dot(hid, wd_ref[c:c + chunk, :])
        acc = part if acc is None else acc + part
        if hooks and n in hooks:
            hooks[n](part)
    return acc


def _ffn_chunk(d_ff):
    for chunk in (512, 256, 128):
        if d_ff % chunk == 0:
            return chunk
    return d_ff


def _mix_ffn_kernel(a_ref, l_ref, x_ref, wo_ref, gain_ref, wg_ref, wu_ref, wd_ref, o_ref, *, chunk):
    sbw = a_ref.shape[1]
    y = _dot(a_ref[...], wo_ref[0:sbw, :]) + _dot(l_ref[...], wo_ref[sbw:, :])
    x1 = x_ref[...] + y
    xb = _rms(x1, gain_ref[...]).astype(BF16)
    o_ref[...] = x1 + _ffn(xb, wg_ref, wu_ref, wd_ref, chunk)


def _mix_ffn(attn, lru_out, x2d, w_out, gain, wg, wu, wd, layer, tm):
    rows, d = x2d.shape
    row = lambda width: pl.BlockSpec((tm, width), lambda i: (i, 0))
    return pl.pallas_call(
        functools.partial(_mix_ffn_kernel, chunk=_ffn_chunk(wg.shape[2])),
        grid=(rows // tm,),
        in_specs=[row(attn.shape[1]), row(lru_out.shape[1]), row(d), _resident(w_out.shape),
                  _resident((1, d)), _layer_of(wg, layer), _layer_of(wu, layer), _layer_of(wd, layer)],
        out_specs=row(d),
        out_shape=jax.ShapeDtypeStruct((rows, d), F32),
        compiler_params=_cparams(1),
        name="mix_ffn",
    )(attn, lru_out, x2d, w_out, gain, wg, wu, wd)


def _pool_delta(ext, n, pos0):
    d = ext.shape[1]
    gw = d // len(POOL_WINDOWS)
    pos = pos0 + lax.broadcasted_iota(jnp.int32, (n, 1), 0)
    cur = ext[POOL_HIST:, :]
    sums, s, col0 = [], ext, 0
    for w in POOL_WINDOWS:
        half = w // 2
        s = s[half:, col0:] + s[:-half, col0:]
        first = POOL_HIST - (w - 1)
        sums.append(s[first:first + n, :gw])
        col0 = gw
    outs = []
    for g, w in enumerate(POOL_WINDOWS):
        inv = 1.0 / jnp.minimum(w, pos + 1).astype(F32)
        outs.append(sums[g] * inv - cur[:, g * gw:(g + 1) * gw])
    return outs


def _pool_steps(gm_ref, pw_ref, ps_ref, gf_ref, wg_ref, wu_ref, wd_ref, gl_ref, d_ref, chunk):
    gw = d_ref.shape[1] // len(POOL_WINDOWS)

    def windows(xr, hist, rows, pos):
        n = xr.shape[0]
        ext = jnp.concatenate([hist, _rms(xr, gm_ref[...])], axis=0)
        for g, dg in enumerate(_pool_delta(ext, n, pos)):
            d_ref[rows, g * gw:(g + 1) * gw] = dg.astype(BF16)
        return ext[n:, :]

    def mixed(xr, rows):
        y = jnp.concatenate([_dot(d_ref[rows, g * gw:(g + 1) * gw], pw_ref[g])
                             for g in range(len(POOL_WINDOWS))], axis=1)
        x1 = xr + y * ps_ref[...]
        return x1, _rms(x1, gf_ref[...]).astype(BF16)

    def ffn_out(x1, xb, hooks=None):
        return _rms(x1 + _ffn(xb, wg_ref, wu_ref, wd_ref, chunk, hooks), gl_ref[...])

    return windows, mixed, ffn_out


def _pool_ffn_kernel(x_ref, buf_ref, gm_ref, pw_ref, ps_ref, gf_ref, wg_ref, wu_ref, wd_ref,
                     gl_ref, o_ref, pn_ref, hist_ref, d_ref, *, chunk, segs_per_batch, start_pos):
    nb, seg, d = x_ref.shape
    windows, mixed, ffn_out = _pool_steps(gm_ref, pw_ref, ps_ref, gf_ref, wg_ref, wu_ref, wd_ref, gl_ref,
                                          d_ref, chunk)
    x = x_ref[...].reshape(nb * seg, d)
    if segs_per_batch == 1:
        seg_idx = 0
    else:
        seg_idx = pl.program_id(0) % segs_per_batch

        @pl.when(seg_idx == 0)
        def _():
            hist_ref[...] = buf_ref[0]

    tails = []
    for b in range(nb):
        rows = slice(b * seg, (b + 1) * seg)
        tails.append(windows(x[rows, :], buf_ref[b] if segs_per_batch == 1 else hist_ref[...], rows,
                             start_pos + seg_idx * seg))
    o_ref[...] = ffn_out(*mixed(x, slice(0, nb * seg))).reshape(nb, seg, d)
    if segs_per_batch == 1:
        for b in range(nb):
            pn_ref[b] = tails[b]
    else:
        hist_ref[...] = tails[0]

        @pl.when(seg_idx == segs_per_batch - 1)
        def _():
            pn_ref[0] = tails[0]


def _pool_ffn_ahead_kernel(x_ref, xn_ref, buf_ref, bufn_ref, gm_ref, pw_ref, ps_ref, gf_ref, wg_ref,
                           wu_ref, wd_ref, gl_ref, o_ref, pn_ref, hist_ref, d_ref, x1a_ref, xba_ref,
                           *, chunk, segs_per_batch, start_pos):
    _, seg, d = x_ref.shape
    half = seg // 2
    rows_a, rows_b = slice(0, half), slice(half, seg)
    windows, mixed, ffn_out = _pool_steps(gm_ref, pw_ref, ps_ref, gf_ref, wg_ref, wu_ref, wd_ref, gl_ref,
                                          d_ref, chunk)
    i = pl.program_id(0)
    seg_idx = i % segs_per_batch
    nxt_idx = (i + 1) % segs_per_batch
    pos0 = start_pos + seg_idx * seg
    n_chunks = wg_ref.shape[1] // chunk
    assert n_chunks >= 2
    x = x_ref[0]

    @pl.when(i == 0)
    def _():
        hist_ref[...] = windows(x[rows_a, :], buf_ref[0], rows_a, pos0)
        x1a_ref[...], xba_ref[...] = mixed(x[rows_a, :], rows_a)

    hist_a = hist_ref[...]
    second, ahead = {}, {}

    def start_b(part):
        second["x"] = _after(x[rows_b, :], part)
        second["hist"] = windows(second["x"], hist_a, rows_b, pos0 + half)

    def mix_b(part):
        second["x1"], second["xb"] = mixed(second["x"], rows_b)

    def start_next(part):
        ahead["x"] = _after(xn_ref[0], part)
        hist = jnp.where(nxt_idx == 0, bufn_ref[0], second["hist"])
        ahead["hist"] = windows(ahead["x"], hist, rows_a, start_pos + nxt_idx * seg)

    def mix_next(part):
        ahead["x1"], ahead["xb"] = mixed(ahead["x"], rows_a)

    o_ref[0, rows_a, :] = ffn_out(x1a_ref[...], xba_ref[...], {0: start_b, n_chunks // 2: mix_b})
    o_ref[0, rows_b, :] = ffn_out(second["x1"], second["xb"], {0: start_next, n_chunks // 2: mix_next})
    x1a_ref[...], xba_ref[...], hist_ref[...] = ahead["x1"], ahead["xb"], ahead["hist"]

    @pl.when(seg_idx == segs_per_batch - 1)
    def _():
        pn_ref[0] = second["hist"]


def _pool_ffn(x2d, pool0, start_pos, gain_mix, pool_w, pool_scale, gain_ffn, wg, wu, wd, layer,
              gain_final, batch, seq, seg, nb):
    d = x2d.shape[1]
    spb = seq // seg
    assert seq % seg == 0 and seg % SUBLANES == 0 and (nb == 1 or spb == 1) and batch % nb == 0
    nseg = batch * spb
    buf = jnp.pad(pool0, ((0, 0), (1, 0), (0, 0)))
    xspec = pl.BlockSpec((nb, seg, d), lambda i: (i, 0, 0))
    bspec = pl.BlockSpec((nb, POOL_HIST, d), (lambda i: (i, 0, 0)) if spb == 1 else (lambda i: (i // spb, 0, 0)))
    vec = _resident((1, d))
    weights = [vec, _resident(pool_w.shape), vec, vec, _layer_of(wg, layer), _layer_of(wu, layer),
               _layer_of(wd, layer), vec]
    weight_args = (gain_mix, pool_w, pool_scale, gain_ffn, wg, wu, wd, gain_final)
    scratch = [pltpu.VMEM((POOL_HIST, d), F32), pltpu.VMEM((nb * seg, d), BF16)]
    static = dict(chunk=_ffn_chunk(wg.shape[2]), segs_per_batch=spb, start_pos=start_pos)
    x3 = x2d.reshape(nseg, seg, d)
    if nb == 1 and seg % BIG_ROW_TILE == 0:
        half = seg // 2
        nxt = lambda i: jnp.minimum(i + 1, nseg - 1)
        kernel_fn = functools.partial(_pool_ffn_ahead_kernel, **static)
        in_specs = [xspec, pl.BlockSpec((1, half, d), lambda i: (2 * nxt(i), 0, 0)), bspec,
                    pl.BlockSpec((1, POOL_HIST, d), lambda i: (nxt(i) // spb, 0, 0))] + weights
        args = (x3, x2d.reshape(2 * nseg, half, d), buf, buf) + weight_args
        scratch = scratch + [pltpu.VMEM((half, d), F32), pltpu.VMEM((half, d), BF16)]
    else:
        kernel_fn = functools.partial(_pool_ffn_kernel, **static)
        in_specs = [xspec, bspec] + weights
        args = (x3, buf) + weight_args
    y, pool_new = pl.pallas_call(
        kernel_fn,
        grid=(nseg // nb,),
        in_specs=in_specs,
        out_specs=[xspec, bspec],
        out_shape=[jax.ShapeDtypeStruct((nseg, seg, d), F32),
                   jax.ShapeDtypeStruct((batch, POOL_HIST, d), F32)],
        scratch_shapes=scratch,
        compiler_params=_cparams(1),
        name="pool_ffn",
    )(*args)
    return y.reshape(batch * seq, d), pool_new[:, 1:, :]


def _pick(n, candidates):
    for c in candidates:
        if n % c == 0:
            return c
    return n


def _gate_pairs(rg_w, ig_w):
    nblk, c, _ = rg_w.shape
    z = jnp.zeros((c, c), rg_w.dtype)

    def pair(w, p):
        return jnp.block([[w[2 * p], z], [z, w[2 * p + 1]]])

    return jnp.stack([jnp.concatenate([pair(rg_w, p), pair(ig_w, p)], axis=1)
                      for p in range(nblk // 2)]).astype(BF16)


def _trunk(x, past_k, past_v, h0, conv0, pool0, start_pos, w):
    batch, seq, d = x.shape
    rows = batch * seq
    x2d = x.reshape(rows, d)
    big = rows >= 4 * BIG_ROW_TILE and seq % BIG_ROW_TILE == 0
    tm = BIG_ROW_TILE if big else _pick(rows, (512, 256, 128, 64, 32, 16, 8))
    fresh = past_k is None

    q, k, v, kb, vb, u, g = _in_proj(x2d, w["norm_mix"][0:1], w["wq"], w["wkv_t"] if fresh else w["wkv"],
                                     w["wug"], tm, batch, seq, keys_on_lanes=fresh)
    q3 = q.reshape(batch, seq, SB_WIDTH)
    ns = _pick(batch, (8, 4, 2, 1))
    if fresh:
        attn = _sb_attention(q3, kb, vb, kb, vb, ns, fresh=True)
    else:
        attn = _sb_attention(q3, kb.reshape(q3.shape), vb.reshape(q3.shape), past_k, past_v, ns, fresh=False)
    attn = attn.reshape(rows, SB_WIDTH)
    d_rnn = u.shape[1]
    lru_out, h_last, conv_new = _lru(u.reshape(batch, seq, d_rnn), g.reshape(batch, seq, d_rnn), h0, conv0,
                                     w["conv_w"], w["conv_b"], w["gate_w"], w["rg_b"], w["ig_b"], w["lam"],
                                     _pick(seq, (128, 64, 32, 16)))
    lru_out = lru_out.reshape(rows, d_rnn)
    x1 = _mix_ffn(attn, lru_out, x2d, w["w_out"], w["norm_ffn"][0:1], w["ffn_gate"], w["ffn_up"],
                  w["ffn_down"], 0, tm)
    if seq >= 512:
        seg, nb = BIG_ROW_TILE if big else _pick(seq, (512, 256, 128)), 1
    else:
        seg, nb = seq, _pick(batch, (8, 4, 2, 1))
    y, pool_new = _pool_ffn(x1, pool0, start_pos, w["norm_mix"][1:2], w["pool_w"], w["pool_scale"],
                            w["norm_ffn"][1:2], w["ffn_gate"], w["ffn_up"], w["ffn_down"], 1,
                            w["norm_final"], batch, seq, seg, nb)
    heads = (N_SB_HEADS, SB_HEAD_DIM)
    if fresh:
        k, v = (t.reshape(1, batch, *heads, seq).transpose(0, 1, 4, 2, 3) for t in (k, v))
    else:
        k, v = (t.reshape(1, batch, seq, *heads) for t in (k, v))
    return (y.reshape(batch, seq, d), k, v, h_last[None], conv_new[None], pool_new[None])


def kernel(x_prompt, x_sample, cache_sb_k, cache_sb_v, state_lru_h, state_lru_conv, state_pool,
           hyb_w_in, hyb_conv_w, hyb_conv_b, hyb_rg_w, hyb_rg_b, hyb_ig_w, hyb_ig_b, hyb_lambda,
           hyb_w_out, pool_w, pool_scale, norm_mix, norm_ffn, ffn_gate, ffn_up, ffn_down, norm_final):
    assert norm_mix.shape[0] == 2 and hyb_w_in.shape[0] == 1 and pool_w.shape[0] == 1
    d = x_prompt.shape[-1]
    d_rnn = hyb_lambda.shape[-1]
    w_in = hyb_w_in[0].astype(BF16)
    w = {
        "wq": w_in[:, 0:SB_WIDTH],
        "wkv": w_in[:, SB_WIDTH:3 * SB_WIDTH],
        "wkv_t": w_in[:, SB_WIDTH:3 * SB_WIDTH].T,
        "wug": w_in[:, 3 * SB_WIDTH:],
        "conv_w": hyb_conv_w[0],
        "conv_b": hyb_conv_b[0].reshape(1, d_rnn),
        "gate_w": _gate_pairs(0.5 * hyb_rg_w[0], 0.5 * hyb_ig_w[0]),
        "rg_b": 0.5 * hyb_rg_b[0].reshape(1, d_rnn),
        "ig_b": 0.5 * hyb_ig_b[0].reshape(1, d_rnn),
        "lam": hyb_lambda[0].reshape(1, d_rnn),
        "w_out": hyb_w_out[0].astype(BF16),
        "pool_w": pool_w[0].astype(BF16),
        "pool_scale": pool_scale[0].reshape(1, d),
        "norm_mix": norm_mix,
        "norm_ffn": norm_ffn,
        "ffn_gate": ffn_gate.astype(BF16),
        "ffn_up": ffn_up.astype(BF16),
        "ffn_down": ffn_down.astype(BF16),
        "norm_final": norm_final.reshape(1, d),
    }
    b = x_prompt.shape[0]
    dt = x_prompt.dtype
    y_p, k_p, v_p, h_p, conv_p, pool_p = _trunk(
        x_prompt, None, None, jnp.zeros((b, d_rnn), dt), jnp.zeros((b, CONV_W - 1, d_rnn), dt),
        jnp.zeros((b, POOL_HIST - 1, d), dt), 0, w)
    db, past = cache_sb_k.shape[1], cache_sb_k.shape[2]
    past_k, past_v = (c[0].transpose(0, 2, 3, 1).reshape(db, SB_WIDTH, past) for c in (cache_sb_k, cache_sb_v))
    y_s, k_s, v_s, h_s, conv_s, pool_s = _trunk(
        x_sample, past_k, past_v, state_lru_h[0], state_lru_conv[0], state_pool[0], past, w)
    return (y_p, y_s, k_p, v_p, h_p, conv_p, pool_p, k_s, v_s, h_s, conv_s, pool_s)
```

```python
import functools

import jax
import jax.numpy as jnp
from jax import lax
from jax.experimental import pallas as pl
from jax.experimental.pallas import tpu as pltpu

F32 = jnp.float32
BF16 = jnp.bfloat16

N_SB_HEADS = 8
SB_HEAD_DIM = 64
SB_WIDTH = N_SB_HEADS * SB_HEAD_DIM
SB_SCALE = SB_HEAD_DIM ** -0.5
CONV_W = 4
LRU_C = 8.0
POOL_WINDOWS = (2, 4, 8, 16)
POOL_HIST = 16
EPS = 1e-6
LOG2_E = 1.4426950408889634
GELU_C = 0.7978845608028654
GELU_CUBIC = 0.044715

LANES = 128
SUBLANES = 8
BF16_ROWS = 16
BIG_ROW_TILE = 1024
HEAD_PAIRS = SB_WIDTH // LANES
KEY_BLOCK = LANES
EXP_ZERO_BELOW = -104.0
VMEM_LIMIT_BYTES = 56 * 1024 * 1024


def _cparams(n_axes):
    return pltpu.CompilerParams(dimension_semantics=("arbitrary",) * n_axes,
                                vmem_limit_bytes=VMEM_LIMIT_BYTES)


def _resident(shape):
    zeros = (0,) * len(shape)
    return pl.BlockSpec(shape, lambda *_: zeros, pipeline_mode=pl.Buffered(1))


def _layer_of(stack, layer):
    return pl.BlockSpec((None,) + stack.shape[1:], lambda *_: (layer, 0, 0), pipeline_mode=pl.Buffered(1))


def _rms(x, gain):
    ms = jnp.mean(x * x, axis=-1, keepdims=True)
    return (x * lax.rsqrt(ms + EPS)) * gain


def _dot(a, b):
    return jnp.dot(a, b, preferred_element_type=F32)


def _dot_nt(a, b):
    return lax.dot_general(a, b, (((1,), (1,)), ((), ())), preferred_element_type=F32)


def _in_proj_kernel(x_ref, gain_ref, wq_ref, wkv_ref, wug_ref, q_ref, k_ref, v_ref, kb_ref, vb_ref,
                    u_ref, g_ref, *, keys_on_lanes):
    xb = _rms(x_ref[...], gain_ref[...]).astype(BF16)
    w = SB_WIDTH
    d_rnn = u_ref.shape[-1]
    g = _dot(xb, wug_ref[:, d_rnn:2 * d_rnn])
    inner = g * (GELU_C + (GELU_C * GELU_CUBIC) * (g * g))
    g_ref[...] = g * (0.5 * jnp.tanh(inner) + 0.5)
    q_ref[...] = (_dot(xb, wq_ref[...]) * SB_SCALE).astype(BF16)
    if keys_on_lanes:
        for dst, dst_b, rows in ((k_ref, kb_ref, slice(0, w)), (v_ref, vb_ref, slice(w, 2 * w))):
            t = _dot_nt(wkv_ref[rows, :], xb)
            dst[...] = t
            for blk in range(dst_b.shape[0]):
                dst_b[blk] = t[:, blk * KEY_BLOCK:(blk + 1) * KEY_BLOCK].astype(BF16)
    else:
        for dst, dst_b, cols in ((k_ref, kb_ref, slice(0, w)), (v_ref, vb_ref, slice(w, 2 * w))):
            t = _dot(xb, wkv_ref[:, cols])
            dst[...] = t
            dst_b[...] = t.astype(BF16)
    u_ref[...] = _dot(xb, wug_ref[:, 0:d_rnn])


def _in_proj(x2d, gain, wq, wkv, wug, tm, batch, seq, keys_on_lanes):
    rows, d = x2d.shape
    d_rnn = wug.shape[1] // 2
    row = lambda width: pl.BlockSpec((tm, width), lambda i: (i, 0))
    if keys_on_lanes:
        assert seq % tm == 0 and tm % KEY_BLOCK == 0
        tpb, kpt = seq // tm, tm // KEY_BLOCK
        kv = pl.BlockSpec((None, SB_WIDTH, tm), lambda i: (i // tpb, 0, i % tpb))
        kvb = pl.BlockSpec((None, kpt, SB_WIDTH, KEY_BLOCK), lambda i: (i // tpb, i % tpb, 0, 0))
        kv_shape = jax.ShapeDtypeStruct((batch, SB_WIDTH, seq), F32)
        kvb_shape = jax.ShapeDtypeStruct((batch, seq // KEY_BLOCK, SB_WIDTH, KEY_BLOCK), BF16)
    else:
        kv = kvb = row(SB_WIDTH)
        kv_shape = jax.ShapeDtypeStruct((rows, SB_WIDTH), F32)
        kvb_shape = jax.ShapeDtypeStruct((rows, SB_WIDTH), BF16)
    return pl.pallas_call(
        functools.partial(_in_proj_kernel, keys_on_lanes=keys_on_lanes),
        grid=(rows // tm,),
        in_specs=[row(d), _resident((1, d)), _resident(wq.shape), _resident(wkv.shape), _resident(wug.shape)],
        out_specs=[row(SB_WIDTH), kv, kv, kvb, kvb, row(d_rnn), row(d_rnn)],
        out_shape=[jax.ShapeDtypeStruct((rows, SB_WIDTH), BF16), kv_shape, kv_shape, kvb_shape, kvb_shape,
                   jax.ShapeDtypeStruct((rows, d_rnn), F32), jax.ShapeDtypeStruct((rows, d_rnn), F32)],
        compiler_params=_cparams(1),
        name="in_proj",
    )(x2d, gain, wq, wkv, wug)


def _cumsum_matrix():
    s = lax.broadcasted_iota(jnp.int32, (KEY_BLOCK, 2 * KEY_BLOCK), 0)
    j = lax.broadcasted_iota(jnp.int32, (KEY_BLOCK, 2 * KEY_BLOCK), 1)
    return ((s > j) | (j >= KEY_BLOCK)).astype(BF16)


def _sb_begin(q_ref, qq_ref, acc_ref, car_ref):
    ns, tq, _ = q_ref.shape
    acc_ref[...] = jnp.zeros_like(acc_ref)
    car_ref[...] = jnp.zeros_like(car_ref)
    low_half = lax.broadcasted_iota(jnp.int32, (tq, LANES), 1) < SB_HEAD_DIM
    for s in range(ns):
        for p in range(HEAD_PAIRS):
            q2 = q_ref[s, :, p * LANES:(p + 1) * LANES]
            zero = jnp.zeros_like(q2)
            qq_ref[s, p, 0:tq, :] = jnp.where(low_half, q2, zero)
            qq_ref[s, p, tq:2 * tq, :] = jnp.where(low_half, zero, q2)


def _sb_tile(qq_ref, kblks, vblks, acc_ref, car_ref, *, diag, keys_on_lanes):
    ns, _, tq2, _ = qq_ref.shape
    tq = tq2 // 2
    assert tq & (tq - 1) == 0
    cm = _cumsum_matrix()
    cm2 = jnp.concatenate([cm, cm], axis=0)
    pair = lambda p: slice(p * LANES, (p + 1) * LANES)
    zs = []
    for s in range(ns):
        for p in range(HEAD_PAIRS):
            if keys_on_lanes:
                zs.append(_dot(qq_ref[s, p], kblks[s][pair(p), :]))
            else:
                zs.append(_dot_nt(qq_ref[s, p], kblks[s][:, pair(p)]))
    z = jnp.concatenate(zs, axis=0)
    soft = jnp.log(1.0 + jnp.exp2(jnp.abs(z) * -LOG2_E))
    log_beta = jnp.minimum(z, 0.0) - soft
    log_not = log_beta - z
    if diag:
        key = lax.broadcasted_iota(jnp.int32, z.shape, 1)
        query = lax.broadcasted_iota(jnp.int32, z.shape, 0) & (tq - 1)
        before = key < query
        log_not = jnp.where(before, log_not, 0.0)
    hi = log_not.astype(BF16)
    lo = (log_not - hi.astype(F32)).astype(BF16)
    sums = _dot(jnp.concatenate([hi, lo], axis=1), cm2)
    car = car_ref[...]
    wgt = jnp.exp(log_beta + sums[:, :KEY_BLOCK] + car)
    if diag:
        wgt = jnp.where(before, wgt, 0.0)
    car_ref[...] = car + sums[:, KEY_BLOCK:]
    wb = wgt.astype(BF16)
    low_half = lax.broadcasted_iota(jnp.int32, (tq, LANES), 1) < SB_HEAD_DIM
    for s in range(ns):
        for p in range(HEAD_PAIRS):
            r0 = (s * HEAD_PAIRS + p) * 2 * tq
            w2 = wb[r0:r0 + 2 * tq, :]
            pv = _dot_nt(w2, vblks[s][pair(p), :]) if keys_on_lanes else _dot(w2, vblks[s][:, pair(p)])
            acc_ref[s, p] += jnp.where(low_half, pv[0:tq, :], pv[tq:2 * tq, :])


def _sb_all_faded(car_ref):
    return jnp.max(car_ref[...]) < EXP_ZERO_BELOW


def _sb_write(o_ref, acc_ref):
    for s in range(acc_ref.shape[0]):
        for p in range(HEAD_PAIRS):
            o_ref[s, :, p * LANES:(p + 1) * LANES] = acc_ref[s, p].astype(o_ref.dtype)


def _sb_scratch(ns, tq):
    return [pltpu.VMEM((ns, HEAD_PAIRS, 2 * tq, LANES), BF16),
            pltpu.VMEM((ns, HEAD_PAIRS, tq, LANES), F32),
            pltpu.VMEM((ns * N_SB_HEADS * tq, LANES), F32)]


def _sb_kernel(q_ref, k0_ref, v0_ref, pk_hbm, pv_hbm, o_ref, qq_ref, acc_ref, car_ref,
               kbuf, vbuf, sem, *, fresh):
    ns = q_ref.shape[0]
    b0 = pl.program_id(0) * ns
    if fresh:
        newest = pl.program_id(1) - 1
        past_block = lambda hbm, s, blk: hbm.at[b0 + s, blk]
    else:
        newest = pk_hbm.shape[2] // KEY_BLOCK - 1
        past_block = lambda hbm, s, blk: hbm.at[b0 + s, :, pl.ds(pl.multiple_of(blk * KEY_BLOCK, KEY_BLOCK),
                                                                 KEY_BLOCK)]

    def copies(blk):
        slot = blk & 1
        return [pltpu.make_async_copy(past_block(hbm, s, blk), buf.at[slot, s], sem.at[a, slot, s])
                for a, (hbm, buf) in enumerate(((pk_hbm, kbuf), (pv_hbm, vbuf))) for s in range(ns)]

    def start(blk):
        for c in copies(blk):
            c.start()

    def wait(blk):
        for c in copies(blk):
            c.wait()

    @pl.when(newest >= 0)
    def _():
        start(newest)

    _sb_begin(q_ref, qq_ref, acc_ref, car_ref)
    if fresh:
        first_k, first_v = [k0_ref[s, 0] for s in range(ns)], [v0_ref[s, 0] for s in range(ns)]
    else:
        first_k, first_v = [k0_ref[s] for s in range(ns)], [v0_ref[s] for s in range(ns)]
    _sb_tile(qq_ref, first_k, first_v, acc_ref, car_ref, diag=True, keys_on_lanes=fresh)

    def more(state):
        blk, faded = state
        return jnp.logical_and(blk >= 0, jnp.logical_not(faded))

    def step(state):
        blk, _ = state
        wait(blk)

        @pl.when(blk >= 1)
        def _():
            start(blk - 1)

        slot = blk & 1
        _sb_tile(qq_ref, [kbuf[slot, s].astype(BF16) for s in range(ns)],
                 [vbuf[slot, s].astype(BF16) for s in range(ns)], acc_ref, car_ref,
                 diag=False, keys_on_lanes=True)
        return blk - 1, _sb_all_faded(car_ref)

    blk, _ = lax.while_loop(more, step, (newest, _sb_all_faded(car_ref)))

    @pl.when(blk >= 0)
    def _():
        wait(blk)

    _sb_write(o_ref, acc_ref)


def _sb_attention(q, k0, v0, past_k, past_v, ns, fresh):
    batch, seq, _ = q.shape
    assert batch % ns == 0
    hbm = pl.BlockSpec(memory_space=pl.ANY)
    if fresh:
        assert seq % KEY_BLOCK == 0
        tq, grid = KEY_BLOCK, (batch // ns, seq // KEY_BLOCK)
        qspec = pl.BlockSpec((ns, tq, SB_WIDTH), lambda b, i: (b, i, 0))
        nspec = pl.BlockSpec((ns, 1, SB_WIDTH, KEY_BLOCK), lambda b, i: (b, i, 0, 0))
    else:
        past = past_k.shape[2]
        assert seq <= KEY_BLOCK and seq % BF16_ROWS == 0 and past % KEY_BLOCK == 0 and past > 0
        pad = ((0, 0), (0, KEY_BLOCK - seq), (0, 0))
        k0, v0 = jnp.pad(k0, pad), jnp.pad(v0, pad)
        tq, grid = seq, (batch // ns,)
        qspec = pl.BlockSpec((ns, tq, SB_WIDTH), lambda b: (b, 0, 0))
        nspec = pl.BlockSpec((ns, KEY_BLOCK, SB_WIDTH), lambda b: (b, 0, 0))
    stage = pltpu.VMEM((2, ns, SB_WIDTH, KEY_BLOCK), past_k.dtype)
    return pl.pallas_call(
        functools.partial(_sb_kernel, fresh=fresh),
        grid=grid,
        in_specs=[qspec, nspec, nspec, hbm, hbm],
        out_specs=qspec,
        out_shape=jax.ShapeDtypeStruct(q.shape, BF16),
        scratch_shapes=_sb_scratch(ns, tq) + [stage, stage, pltpu.SemaphoreType.DMA((2, 2, ns))],
        compiler_params=_cparams(len(grid)),
        name="sb_prompt" if fresh else "sb_sample",
    )(q, k0, v0, past_k, past_v)


def _lru_kernel(u_ref, g_ref, h0_ref, c0_ref, cw_ref, cb_ref, gw_ref, rb_ref, ib_ref, lam_ref,
                o_ref, hl_ref, cn_ref, h_ref, ext_ref, a_ref, x_ref, hseq_ref):
    i = pl.program_id(1)
    nb, tc, d = u_ref.shape
    n = nb * tc
    hist = SUBLANES

    @pl.when(i == 0)
    def _():
        h_ref[...] = h0_ref[...]
        ext_ref[:, 0:hist, :] = c0_ref[...]

    u3 = u_ref[...]
    ext_ref[:, hist:, :] = u3
    uc = cb_ref[...] + u3.reshape(n, d) * cw_ref[CONV_W - 1:CONV_W, :]
    for back in range(1, CONV_W):
        prev = ext_ref[:, hist - back:hist - back + tc, :].reshape(n, d)
        uc = uc + prev * cw_ref[CONV_W - 1 - back:CONV_W - back, :]
    tail = ext_ref[:, tc:tc + hist, :]
    ext_ref[:, 0:hist, :] = tail

    ucb = uc.astype(BF16)
    gates = [_dot(ucb[:, p * LANES:(p + 1) * LANES], gw_ref[p]) for p in range(d // LANES)]
    tr = jnp.tanh(jnp.concatenate([gt[:, :LANES] for gt in gates], axis=1) + rb_ref[...])
    ti = jnp.tanh(jnp.concatenate([gt[:, LANES:] for gt in gates], axis=1) + ib_ref[...])
    lam = lam_ref[...]
    softplus_neg = jnp.maximum(-lam, 0.0) + jnp.log1p(jnp.exp(-jnp.abs(lam)))
    log_a = (tr + 1.0) * ((-0.5 * LRU_C) * softplus_neg)
    a = jnp.exp(log_a)
    th = jnp.tanh(log_a)
    p = -2.0 * th
    mult = p * lax.rsqrt(jnp.maximum(p * (1.0 - th), jnp.finfo(F32).tiny))
    x_in = (mult * uc) * (0.5 * ti + 0.5)
    nlb = d // LANES
    pitch = a_ref.shape[1] // nb
    for c in range(nlb):
        for b in range(nb):
            dst = slice(b * pitch, b * pitch + tc)
            a_ref[c, dst, :] = a[b * tc:(b + 1) * tc, c * LANES:(c + 1) * LANES]
            x_ref[c, dst, :] = x_in[b * tc:(b + 1) * tc, c * LANES:(c + 1) * LANES]

    hs = [h_ref[:, c * LANES:(c + 1) * LANES] for c in range(nlb)]
    for t in range(tc):
        step = pl.ds(t, nb, stride=pitch)
        for c in range(nlb):
            hs[c] = a_ref[c, step, :] * hs[c] + x_ref[c, step, :]
            hseq_ref[c, step, :] = hs[c]
    h = jnp.concatenate(hs, axis=1)
    h_ref[...] = h
    h_all = jnp.concatenate(
        [jnp.concatenate([hseq_ref[c, b * pitch:b * pitch + tc, :] for b in range(nb)], axis=0)
         for c in range(nlb)], axis=1)

    o_ref[...] = (h_all * g_ref[...].reshape(n, d)).astype(o_ref.dtype).reshape(nb, tc, d)

    @pl.when(i == pl.num_programs(1) - 1)
    def _():
        hl_ref[...] = h
        cn_ref[...] = tail


def _lru(u, g, h0, conv0, conv_w, conv_b, gate_w, rg_b, ig_b, lam, tc):
    batch, seq, d_rnn = u.shape
    nb = SUBLANES
    assert batch % nb == 0 and seq % tc == 0 and tc % BF16_ROWS == 0 and seq >= CONV_W - 1
    conv0p = jnp.pad(conv0, ((0, 0), (SUBLANES - (CONV_W - 1), 0), (0, 0)))
    chunk = pl.BlockSpec((nb, tc, d_rnn), lambda b, i: (b, i, 0))
    state = pl.BlockSpec((nb, d_rnn), lambda b, i: (b, 0))
    tails = pl.BlockSpec((nb, SUBLANES, d_rnn), lambda b, i: (b, 0, 0))
    vec = _resident((1, d_rnn))
    pitch = tc + SUBLANES if (tc // SUBLANES) % 2 == 0 else tc
    scan_buf = pltpu.VMEM((d_rnn // LANES, nb * pitch, LANES), F32)
    out, h_last, conv_new = pl.pallas_call(
        _lru_kernel,
        grid=(batch // nb, seq // tc),
        in_specs=[chunk, chunk, state, tails, _resident(conv_w.shape), vec,
                  _resident(gate_w.shape), vec, vec, vec],
        out_specs=[chunk, state, tails],
        out_shape=[jax.ShapeDtypeStruct(u.shape, BF16),
                   jax.ShapeDtypeStruct((batch, d_rnn), F32),
                   jax.ShapeDtypeStruct((batch, SUBLANES, d_rnn), F32)],
        scratch_shapes=[pltpu.VMEM((nb, d_rnn), F32),
                        pltpu.VMEM((nb, SUBLANES + tc, d_rnn), F32),
                        scan_buf, scan_buf, scan_buf],
        compiler_params=_cparams(2),
        name="rg_lru",
    )(u, g, h0, conv0p, conv_w, conv_b, gate_w, rg_b, ig_b, lam)
    return out, h_last, conv_new[:, SUBLANES - (CONV_W - 1):, :]


def _after(x, anchor):
    bits = lax.bitcast_convert_type(anchor[0:1, 0:LANES], jnp.uint32)
    zero = lax.shift_right_logical(lax.shift_right_logical(bits, jnp.uint32(16)), jnp.uint32(16)).astype(F32)
    return x + jnp.concatenate([zero] * (x.shape[1] // LANES), axis=1)


def _ffn(xb, wg_ref, wu_ref, wd_ref, chunk, hooks=None):
    d_ff = wg_ref.shape[1]
    acc = None
    for n, c in enumerate(range(0, d_ff, chunk)):
        gate = _dot(xb, wg_ref[:, c:c + chunk])
        up = _dot(xb, wu_ref[:, c:c + chunk])
        hid = (gate * jax.nn.sigmoid(gate) * up).astype(BF16)
        part = _dot(hid, wd_ref[c:c + chunk, :])
        acc = part if acc is None else acc + part
        if hooks and n in hooks:
            hooks[n](part)
    return acc


def _ffn_chunk(d_ff):
    for chunk in (512, 256, 128):
        if d_ff % chunk == 0:
            return chunk
    return d_ff


def _mix_ffn_kernel(a_ref, l_ref, x_ref, wo_ref, gain_ref, wg_ref, wu_ref, wd_ref, o_ref, *, chunk):
    sbw = a_ref.shape[1]
    y = _dot(a_ref[...], wo_ref[0:sbw, :]) + _dot(l_ref[...], wo_ref[sbw:, :])
    x1 = x_ref[...] + y
    xb = _rms(x1, gain_ref[...]).astype(BF16)
    o_ref[...] = x1 + _ffn(xb, wg_ref, wu_ref, wd_ref, chunk)


def _mix_ffn(attn, lru_out, x2d, w_out, gain, wg, wu, wd, layer, tm):
    rows, d = x2d.shape
    row = lambda width: pl.BlockSpec((tm, width), lambda i: (i, 0))
    return pl.pallas_call(
        functools.partial(_mix_ffn_kernel, chunk=_ffn_chunk(wg.shape[2])),
        grid=(rows // tm,),
        in_specs=[row(attn.shape[1]), row(lru_out.shape[1]), row(d), _resident(w_out.shape),
                  _resident((1, d)), _layer_of(wg, layer), _layer_of(wu, layer), _layer_of(wd, layer)],
        out_specs=row(d),
        out_shape=jax.ShapeDtypeStruct((rows, d), F32),
        compiler_params=_cparams(1),
        name="mix_ffn",
    )(attn, lru_out, x2d, w_out, gain, wg, wu, wd)


def _pool_delta(ext, n, pos0):
    d = ext.shape[1]
    gw = d // len(POOL_WINDOWS)
    pos = pos0 + lax.broadcasted_iota(jnp.int32, (n, 1), 0)
    cur = ext[POOL_HIST:, :]
    sums, s, col0 = [], ext, 0
    for w in POOL_WINDOWS:
        half = w // 2
        s = s[half:, col0:] + s[:-half, col0:]
        first = POOL_HIST - (w - 1)
        sums.append(s[first:first + n, :gw])
        col0 = gw
    outs = []
    for g, w in enumerate(POOL_WINDOWS):
        inv = 1.0 / jnp.minimum(w, pos + 1).astype(F32)
        outs.append(sums[g] * inv - cur[:, g * gw:(g + 1) * gw])
    return outs


def _pool_steps(gm_ref, pw_ref, ps_ref, gf_ref, wg_ref, wu_ref, wd_ref, gl_ref, d_ref, chunk):
    gw = d_ref.shape[1] // len(POOL_WINDOWS)

    def windows(xr, hist, rows, pos):
        n = xr.shape[0]
        ext = jnp.concatenate([hist, _rms(xr, gm_ref[...])], axis=0)
        for g, dg in enumerate(_pool_delta(ext, n, pos)):
            d_ref[rows, g * gw:(g + 1) * gw] = dg.astype(BF16)
        return ext[n:, :]

    def mixed(xr, rows):
        y = jnp.concatenate([_dot(d_ref[rows, g * gw:(g + 1) * gw], pw_ref[g])
                             for g in range(len(POOL_WINDOWS))], axis=1)
        x1 = xr + y * ps_ref[...]
        return x1, _rms(x1, gf_ref[...]).astype(BF16)

    def ffn_out(x1, xb, hooks=None):
        return _rms(x1 + _ffn(xb, wg_ref, wu_ref, wd_ref, chunk, hooks), gl_ref[...])

    return windows, mixed, ffn_out


def _pool_ffn_kernel(x_ref, buf_ref, gm_ref, pw_ref, ps_ref, gf_ref, wg_ref, wu_ref, wd_ref,
                     gl_ref, o_ref, pn_ref, hist_ref, d_ref, *, chunk, segs_per_batch, start_pos):
    nb, seg, d = x_ref.shape
    windows, mixed, ffn_out = _pool_steps(gm_ref, pw_ref, ps_ref, gf_ref, wg_ref, wu_ref, wd_ref, gl_ref,
                                          d_ref, chunk)
    x = x_ref[...].reshape(nb * seg, d)
    if segs_per_batch == 1:
        seg_idx = 0
    else:
        seg_idx = pl.program_id(0) % segs_per_batch

        @pl.when(seg_idx == 0)
        def _():
            hist_ref[...] = buf_ref[0]

    tails = []
    for b in range(nb):
        rows = slice(b * seg, (b + 1) * seg)
        tails.append(windows(x[rows, :], buf_ref[b] if segs_per_batch == 1 else hist_ref[...], rows,
                             start_pos + seg_idx * seg))
    o_ref[...] = ffn_out(*mixed(x, slice(0, nb * seg))).reshape(nb, seg, d)
    if segs_per_batch == 1:
        for b in range(nb):
            pn_ref[b] = tails[b]
    else:
        hist_ref[...] = tails[0]

        @pl.when(seg_idx == segs_per_batch - 1)
        def _():
            pn_ref[0] = tails[0]


def _pool_ffn_ahead_kernel(x_ref, xn_ref, buf_ref, bufn_ref, gm_ref, pw_ref, ps_ref, gf_ref, wg_ref,
                           wu_ref, wd_ref, gl_ref, o_ref, pn_ref, hist_ref, d_ref, x1a_ref, xba_ref,
                           *, chunk, segs_per_batch, start_pos):
    _, seg, d = x_ref.shape
    half = seg // 2
    rows_a, rows_b = slice(0, half), slice(half, seg)
    windows, mixed, ffn_out = _pool_steps(gm_ref, pw_ref, ps_ref, gf_ref, wg_ref, wu_ref, wd_ref, gl_ref,
                                          d_ref, chunk)
    i = pl.program_id(0)
    seg_idx = i % segs_per_batch
    nxt_idx = (i + 1) % segs_per_batch
    pos0 = start_pos + seg_idx * seg
    n_chunks = wg_ref.shape[1] // chunk
    assert n_chunks >= 2
    x = x_ref[0]

    @pl.when(i == 0)
    def _():
        hist_ref[...] = windows(x[rows_a, :], buf_ref[0], rows_a, pos0)
        x1a_ref[...], xba_ref[...] = mixed(x[rows_a, :], rows_a)

    hist_a = hist_ref[...]
    second, ahead = {}, {}

    def start_b(part):
        second["x"] = _after(x[rows_b, :], part)
        second["hist"] = windows(second["x"], hist_a, rows_b, pos0 + half)

    def mix_b(part):
        second["x1"], second["xb"] = mixed(second["x"], rows_b)

    def start_next(part):
        ahead["x"] = _after(xn_ref[0], part)
        hist = jnp.where(nxt_idx == 0, bufn_ref[0], second["hist"])
        ahead["hist"] = windows(ahead["x"], hist, rows_a, start_pos + nxt_idx * seg)

    def mix_next(part):
        ahead["x1"], ahead["xb"] = mixed(ahead["x"], rows_a)

    o_ref[0, rows_a, :] = ffn_out(x1a_ref[...], xba_ref[...], {0: start_b, n_chunks // 2: mix_b})
    o_ref[0, rows_b, :] = ffn_out(second["x1"], second["xb"], {0: start_next, n_chunks // 2: mix_next})
    x1a_ref[...], xba_ref[...], hist_ref[...] = ahead["x1"], ahead["xb"], ahead["hist"]

    @pl.when(seg_idx == segs_per_batch - 1)
    def _():
        pn_ref[0] = second["hist"]


def _pool_ffn(x2d, pool0, start_pos, gain_mix, pool_w, pool_scale, gain_ffn, wg, wu, wd, layer,
              gain_final, batch, seq, seg, nb):
    d = x2d.shape[1]
    spb = seq // seg
    assert seq % seg == 0 and seg % SUBLANES == 0 and (nb == 1 or spb == 1) and batch % nb == 0
    nseg = batch * spb
    buf = jnp.pad(pool0, ((0, 0), (1, 0), (0, 0)))
    xspec = pl.BlockSpec((nb, seg, d), lambda i: (i, 0, 0))
    bspec = pl.BlockSpec((nb, POOL_HIST, d), (lambda i: (i, 0, 0)) if spb == 1 else (lambda i: (i // spb, 0, 0)))
    vec = _resident((1, d))
    weights = [vec, _resident(pool_w.shape), vec, vec, _layer_of(wg, layer), _layer_of(wu, layer),
               _layer_of(wd, layer), vec]
    weight_args = (gain_mix, pool_w, pool_scale, gain_ffn, wg, wu, wd, gain_final)
    scratch = [pltpu.VMEM((POOL_HIST, d), F32), pltpu.VMEM((nb * seg, d), BF16)]
    static = dict(chunk=_ffn_chunk(wg.shape[2]), segs_per_batch=spb, start_pos=start_pos)
    x3 = x2d.reshape(nseg, seg, d)
    if nb == 1 and seg % BIG_ROW_TILE == 0:
        half = seg // 2
        nxt = lambda i: jnp.minimum(i + 1, nseg - 1)
        kernel_fn = functools.partial(_pool_ffn_ahead_kernel, **static)
        in_specs = [xspec, pl.BlockSpec((1, half, d), lambda i: (2 * nxt(i), 0, 0)), bspec,
                    pl.BlockSpec((1, POOL_HIST, d), lambda i: (nxt(i) // spb, 0, 0))] + weights
        args = (x3, x2d.reshape(2 * nseg, half, d), buf, buf) + weight_args
        scratch = scratch + [pltpu.VMEM((half, d), F32), pltpu.VMEM((half, d), BF16)]
    else:
        kernel_fn = functools.partial(_pool_ffn_kernel, **static)
        in_specs = [xspec, bspec] + weights
        args = (x3, buf) + weight_args
    y, pool_new = pl.pallas_call(
        kernel_fn,
        grid=(nseg // nb,),
        in_specs=in_specs,
        out_specs=[xspec, bspec],
        out_shape=[jax.ShapeDtypeStruct((nseg, seg, d), F32),
                   jax.ShapeDtypeStruct((batch, POOL_HIST, d), F32)],
        scratch_shapes=scratch,
        compiler_params=_cparams(1),
        name="pool_ffn",
    )(*args)
    return y.reshape(batch * seq, d), pool_new[:, 1:, :]


def _pick(n, candidates):
    for c in candidates:
        if n % c == 0:
            return c
    return n


def _gate_pairs(rg_w, ig_w):
    nblk, c, _ = rg_w.shape
    z = jnp.zeros((c, c), rg_w.dtype)

    def pair(w, p):
        return jnp.block([[w[2 * p], z], [z, w[2 * p + 1]]])

    return jnp.stack([jnp.concatenate([pair(rg_w, p), pair(ig_w, p)], axis=1)
                      for p in range(nblk // 2)]).astype(BF16)


def _trunk(x, past_k, past_v, h0, conv0, pool0, start_pos, w):
    batch, seq, d = x.shape
    rows = batch * seq
    x2d = x.reshape(rows, d)
    big = rows >= 4 * BIG_ROW_TILE and seq % BIG_ROW_TILE == 0
    tm = BIG_ROW_TILE if big else _pick(rows, (512, 256, 128, 64, 32, 16, 8))
    fresh = past_k is None

    q, k, v, kb, vb, u, g = _in_proj(x2d, w["norm_mix"][0:1], w["wq"], w["wkv_t"] if fresh else w["wkv"],
                                     w["wug"], tm, batch, seq, keys_on_lanes=fresh)
    q3 = q.reshape(batch, seq, SB_WIDTH)
    ns = _pick(batch, (8, 4, 2, 1))
    if fresh:
        attn = _sb_attention(q3, kb, vb, kb, vb, ns, fresh=True)
    else:
        attn = _sb_attention(q3, kb.reshape(q3.shape), vb.reshape(q3.shape), past_k, past_v, ns, fresh=False)
    attn = attn.reshape(rows, SB_WIDTH)
    d_rnn = u.shape[1]
    lru_out, h_last, conv_new = _lru(u.reshape(batch, seq, d_rnn), g.reshape(batch, seq, d_rnn), h0, conv0,
                                     w["conv_w"], w["conv_b"], w["gate_w"], w["rg_b"], w["ig_b"], w["lam"],
                                     _pick(seq, (256, 128, 64, 32, 16)))
    lru_out = lru_out.reshape(rows, d_rnn)
    x1 = _mix_ffn(attn, lru_out, x2d, w["w_out"], w["norm_ffn"][0:1], w["ffn_gate"], w["ffn_up"],
                  w["ffn_down"], 0, tm)
    if seq >= 512:
        seg, nb = BIG_ROW_TILE if big else _pick(seq, (512, 256, 128)), 1
    else:
        seg, nb = seq, _pick(batch, (16, 8, 4, 2, 1))
    y, pool_new = _pool_ffn(x1, pool0, start_pos, w["norm_mix"][1:2], w["pool_w"], w["pool_scale"],
                            w["norm_ffn"][1:2], w["ffn_gate"], w["ffn_up"], w["ffn_down"], 1,
                            w["norm_final"], batch, seq, seg, nb)
    heads = (N_SB_HEADS, SB_HEAD_DIM)
    if fresh:
        k, v = (t.reshape(1, batch, *heads, seq).transpose(0, 1, 4, 2, 3) for t in (k, v))
    else:
        k, v = (t.reshape(1, batch, seq, *heads) for t in (k, v))
    return (y.reshape(batch, seq, d), k, v, h_last[None], conv_new[None], pool_new[None])


def kernel(x_prompt, x_sample, cache_sb_k, cache_sb_v, state_lru_h, state_lru_conv, state_pool,
           hyb_w_in, hyb_conv_w, hyb_conv_b, hyb_rg_w, hyb_rg_b, hyb_ig_w, hyb_ig_b, hyb_lambda,
           hyb_w_out, pool_w, pool_scale, norm_mix, norm_ffn, ffn_gate, ffn_up, ffn_down, norm_final):
    assert norm_mix.shape[0] == 2 and hyb_w_in.shape[0] == 1 and pool_w.shape[0] == 1
    d = x_prompt.shape[-1]
    d_rnn = hyb_lambda.shape[-1]
    w_in = hyb_w_in[0].astype(BF16)
    w = {
        "wq": w_in[:, 0:SB_WIDTH],
        "wkv": w_in[:, SB_WIDTH:3 * SB_WIDTH],
        "wkv_t": w_in[:, SB_WIDTH:3 * SB_WIDTH].T,
        "wug": w_in[:, 3 * SB_WIDTH:],
        "conv_w": hyb_conv_w[0],
        "conv_b": hyb_conv_b[0].reshape(1, d_rnn),
        "gate_w": _gate_pairs(0.5 * hyb_rg_w[0], 0.5 * hyb_ig_w[0]),
        "rg_b": 0.5 * hyb_rg_b[0].reshape(1, d_rnn),
        "ig_b": 0.5 * hyb_ig_b[0].reshape(1, d_rnn),
        "lam": hyb_lambda[0].reshape(1, d_rnn),
        "w_out": hyb_w_out[0].astype(BF16),
        "pool_w": pool_w[0].astype(BF16),
        "pool_scale": pool_scale[0].reshape(1, d),
        "norm_mix": norm_mix,
        "norm_ffn": norm_ffn,
        "ffn_gate": ffn_gate.astype(BF16),
        "ffn_up": ffn_up.astype(BF16),
        "ffn_down": ffn_down.astype(BF16),
        "norm_final": norm_final.reshape(1, d),
    }
    b = x_prompt.shape[0]
    dt = x_prompt.dtype
    y_p, k_p, v_p, h_p, conv_p, pool_p = _trunk(
        x_prompt, None, None, jnp.zeros((b, d_rnn), dt), jnp.zeros((b, CONV_W - 1, d_rnn), dt),
        jnp.zeros((b, POOL_HIST - 1, d), dt), 0, w)
    db, past = cache_sb_k.shape[1], cache_sb_k.shape[2]
    past_k, past_v = (c[0].transpose(0, 2, 3, 1).reshape(db, SB_WIDTH, past) for c in (cache_sb_k, cache_sb_v))
    y_s, k_s, v_s, h_s, conv_s, pool_s = _trunk(
        x_sample, past_k, past_v, state_lru_h[0], state_lru_conv[0], state_pool[0], past, w)
    return (y_p, y_s, k_p, v_p, h_p, conv_p, pool_p, k_s, v_s, h_s, conv_s, pool_s)
```
